```python
import math
import jax
import jax.numpy as jnp
from jax import lax
import numpy as np

D_MODEL = 1024
BATCH = 16
SEQ = 2048
DEPTH = 4

N_MIXERS = 4
EPS = 1e-6
F32 = jnp.float32
OUT_SCALE = 0.5
A_HEADS = 4
A_DQK = 128
A_DV = 256
A_CHUNK = 64
A_IN = 2 * A_HEADS * A_DQK + 2 * A_HEADS * A_DV + 2 * A_HEADS
B_HEADS = 8
B_DH = 128
B_BLOCK = 256
B_TOPK = 3
B_QBLOCK = 128
ROPE_THETA = 500000.0
ROPE_DIMS = B_DH // 4
B_IN = 3 * B_HEADS * B_DH
C_HEADS = 4
C_DK = D_MODEL // C_HEADS
C_DV = 2 * C_DK
C_CHUNK = 128
C_ROPE_THETA = 10000.0
C_IN = 2 * C_HEADS * C_DK + 2 * C_HEADS * C_DV
D_RNN = D_MODEL
D_BLOCKS = 4
D_BW = D_RNN // D_BLOCKS
D_CONV = 4
LRU_C = 8.0
D_FF = ((8 * D_MODEL // 3 + 255) // 256) * 256
N_LAYERS_A = len(range(0, DEPTH, N_MIXERS))
N_LAYERS_B = len(range(1, DEPTH, N_MIXERS))
N_LAYERS_C = len(range(2, DEPTH, N_MIXERS))
N_LAYERS_D = len(range(3, DEPTH, N_MIXERS))

kernel_name = 'interleaved_mlstm_moba_retention_rglru_trunk'


def rmsnorm(x, g):
    xf = x.astype(F32)
    y = xf * lax.rsqrt(jnp.mean(xf * xf, axis=-1, keepdims=True) + EPS)
    return (y * g.astype(F32)).astype(x.dtype)


def rotary(x, pos, n_rot, theta):
    half = n_rot // 2
    inv = theta ** (-jnp.arange(half, dtype=F32) * (2.0 / n_rot))
    ang = pos.astype(F32)[:, None, :, None] * inv
    cos, sin = jnp.cos(ang), jnp.sin(ang)
    xf = x.astype(F32)
    x1, x2 = xf[..., :half], xf[..., half:n_rot]
    out = jnp.concatenate([x1 * cos - x2 * sin, x2 * cos + x1 * sin, xf[..., n_rot:]], axis=-1)
    return out.astype(x.dtype)


def split_heads(t, h):
    b, s, _ = t.shape
    return t.reshape(b, s, h, -1).transpose(0, 2, 1, 3)


def merge_heads(t):
    b, h, s, d = t.shape
    return t.transpose(0, 2, 1, 3).reshape(b, s, h * d)


def to_chunks(t, l):
    b, h, s, d = t.shape
    return t.reshape(b, h, s // l, l, d).transpose(2, 0, 1, 3, 4)


def from_chunks(t):
    nc, b, h, l, d = t.shape
    return t.transpose(1, 2, 0, 3, 4).reshape(b, h, nc * l, d)


def mlstm_mixer(u, w_in, b_gates, g_out, w_out):
    b, s, _ = u.shape
    h, l = A_HEADS, A_CHUNK
    nc = s // l
    sq, sv = h * A_DQK, h * A_DV
    q, k, v, o, gl = jnp.split(u @ w_in, [sq, 2 * sq, 2 * sq + sv, 2 * sq + 2 * sv], axis=-1)
    gl = gl.astype(F32) + b_gates.astype(F32)
    log_i = gl[..., :h]
    log_f = jax.nn.log_sigmoid(gl[..., h:])
    qh = split_heads(q, h).astype(F32) * (A_DQK ** -0.5)
    kh = split_heads(k, h).astype(F32)
    vh = split_heads(v, h).astype(F32)

    def gate_chunks(g):
        return g.transpose(0, 2, 1).reshape(b, h, nc, l).transpose(2, 0, 1, 3)

    causal = jnp.tril(jnp.ones((l, l), dtype=bool))

    def step(carry, xs):
        c_st, n_st, m_st = carry
        qc, kc, vc, ic, fc = xs
        bcum = jnp.cumsum(fc, axis=-1)
        btot = bcum[..., -1]
        dmat = jnp.where(causal, bcum[..., :, None] - bcum[..., None, :] + ic[..., None, :], -jnp.inf)
        inter = bcum + m_st[..., None]
        m_t = jnp.maximum(inter, jnp.max(dmat, axis=-1))
        w_intra = jnp.exp(dmat - m_t[..., None])
        w_inter = jnp.exp(inter - m_t)
        qk = jnp.einsum('bhtd,bhsd->bhts', qc, kc) * w_intra
        num = jnp.einsum('bhts,bhsv->bhtv', qk, vc) + w_inter[..., None] * jnp.einsum('bhvd,bhtd->bhtv', c_st, qc)
        den = jnp.sum(qk, axis=-1) + w_inter * jnp.einsum('bhd,bhtd->bht', n_st, qc)
        h_out = num / jnp.maximum(jnp.abs(den), jnp.exp(-m_t))[..., None]
        dec = btot[..., None] - bcum + ic
        m_new = jnp.maximum(btot + m_st, jnp.max(dec, axis=-1))
        ws = jnp.exp(dec - m_new[..., None])
        wc = jnp.exp(btot + m_st - m_new)
        c_new = wc[..., None, None] * c_st + jnp.einsum('bhs,bhsv,bhsd->bhvd', ws, vc, kc)
        n_new = wc[..., None] * n_st + jnp.einsum('bhs,bhsd->bhd', ws, kc)
        return (c_new, n_new, m_new), h_out

    init = (jnp.zeros((b, h, A_DV, A_DQK), F32), jnp.zeros((b, h, A_DQK), F32), jnp.zeros((b, h), F32))
    xs = (to_chunks(qh, l), to_chunks(kh, l), to_chunks(vh, l), gate_chunks(log_i), gate_chunks(log_f))
    _, hc = lax.scan(step, init, xs)
    hs = rmsnorm(from_chunks(hc), g_out[:, None, :])
    y = merge_heads(hs).astype(u.dtype) * jax.nn.sigmoid(o)
    return y @ w_out


def moba_mixer(u, positions, w_in, g_q, g_k, w_out):
    b, s, _ = u.shape
    h, dh = B_HEADS, B_DH
    q, k, v = jnp.split(u @ w_in, 3, axis=-1)
    qh = rotary(rmsnorm(split_heads(q, h), g_q), positions, ROPE_DIMS, ROPE_THETA)
    kh = rotary(rmsnorm(split_heads(k, h), g_k), positions, ROPE_DIMS, ROPE_THETA)
    vh = split_heads(v, h)
    nblk = -(-s // B_BLOCK)
    sp = nblk * B_BLOCK
    pad = ((0, 0), (0, 0), (0, sp - s), (0, 0))
    qh, kh, vh = jnp.pad(qh, pad), jnp.pad(kh, pad), jnp.pad(vh, pad)
    kb = kh.reshape(b, h, nblk, B_BLOCK, dh)
    vb = vh.reshape(b, h, nblk, B_BLOCK, dh)
    kmean = jnp.mean(kb.astype(F32), axis=3)
    nqb = sp // B_QBLOCK
    per_blk = B_BLOCK // B_QBLOCK
    topk = min(B_TOPK, nblk - 1)
    qblocks = qh.reshape(b, h, nqb, B_QBLOCK, dh).transpose(0, 2, 1, 3, 4).reshape(b * nqb, h, B_QBLOCK, dh)
    scale = dh ** -0.5
    qoff = jnp.arange(B_QBLOCK)
    koff = jnp.arange(B_BLOCK)

    def attend(args):
        idx, qc = args
        bi = idx // nqb
        qi = idx % nqb
        j = qi // per_blk
        kb_b, vb_b = kb[bi], vb[bi]
        qf = qc.astype(F32) * scale
        qpos = qi * B_QBLOCK + qoff
        kpos = j * B_BLOCK + koff
        k_own = kb_b[:, j].astype(F32)
        v_own = vb_b[:, j].astype(F32)
        s_own = jnp.einsum('hqd,hnd->hqn', qf, k_own)
        s_own = jnp.where(kpos[None, None, :] <= qpos[None, :, None], s_own, -jnp.inf)
        if topk > 0:
            gate = jnp.einsum('hqd,hnd->hqn', qf, kmean[bi])
            gate = jnp.where(jnp.arange(nblk)[None, None, :] < j, gate, -jnp.inf)
            _, sel = lax.top_k(gate, topk)
            k_sel = jax.vmap(lambda kh_, ih_: kh_[ih_])(kb_b, sel).astype(F32)
            v_sel = jax.vmap(lambda vh_, ih_: vh_[ih_])(vb_b, sel).astype(F32)
            s_sel = jnp.einsum('hqd,hqtnd->hqtn', qf, k_sel)
            valid = jnp.arange(topk) < j
            s_sel = jnp.where(valid[None, None, :, None], s_sel, -jnp.inf)
            logits = jnp.concatenate([s_sel.reshape(h, B_QBLOCK, topk * B_BLOCK), s_own], axis=-1)
            p = jax.nn.softmax(logits, axis=-1)
            p_sel = p[..., :topk * B_BLOCK].reshape(h, B_QBLOCK, topk, B_BLOCK)
            out = jnp.einsum('hqtn,hqtnd->hqd', p_sel, v_sel) + jnp.einsum('hqn,hnd->hqd', p[..., topk * B_BLOCK:], v_own)
        else:
            p = jax.nn.softmax(s_own, axis=-1)
            out = jnp.einsum('hqn,hnd->hqd', p, v_own)
        return out.astype(qc.dtype)

    out = lax.map(attend, (jnp.arange(b * nqb, dtype=jnp.int32), qblocks))
    out = out.reshape(b, nqb, h, B_QBLOCK, dh).transpose(0, 2, 1, 3, 4).reshape(b, h, sp, dh)[:, :, :s]
    return merge_heads(out) @ w_out


def retention_mixer(u, positions, w_in, g_out, w_out):
    b, s, _ = u.shape
    h, l = C_HEADS, C_CHUNK
    sk, sv = h * C_DK, h * C_DV
    q, k, v, g = jnp.split(u @ w_in, [sk, 2 * sk, 2 * sk + sv], axis=-1)
    qh = rotary(split_heads(q, h), positions, C_DK, C_ROPE_THETA).astype(F32)
    kh = rotary(split_heads(k, h), positions, C_DK, C_ROPE_THETA).astype(F32) * (C_DK ** -0.5)
    vh = split_heads(v, h).astype(F32)
    log_g = jnp.log1p(-jnp.exp2(-5.0 - jnp.arange(h, dtype=F32)))
    idx = jnp.arange(l, dtype=F32)
    diff = idx[:, None] - idx[None, :]
    intra = jnp.where(diff >= 0, jnp.exp(jnp.maximum(diff, 0.0)[None] * log_g[:, None, None]), 0.0)
    xi = jnp.exp((idx + 1.0)[None, :] * log_g[:, None])
    zeta = jnp.exp((l - 1.0 - idx)[None, :] * log_g[:, None])
    g_chunk = jnp.exp(l * log_g)

    def step(r_st, xs):
        qc, kc, vc = xs
        inner = jnp.einsum('bhts,bhsv->bhtv', jnp.einsum('bhtd,bhsd->bhts', qc, kc) * intra, vc)
        cross = jnp.einsum('bhtd,bhdv->bhtv', qc, r_st) * xi[None, :, :, None]
        r_new = g_chunk[None, :, None, None] * r_st + jnp.einsum('bhsd,bhsv->bhdv', kc * zeta[None, :, :, None], vc)
        return r_new, inner + cross

    _, yc = lax.scan(step, jnp.zeros((b, h, C_DK, C_DV), F32), (to_chunks(qh, l), to_chunks(kh, l), to_chunks(vh, l)))
    ys = from_chunks(yc)
    mu = jnp.mean(ys, axis=-1, keepdims=True)
    var = jnp.mean(jnp.square(ys - mu), axis=-1, keepdims=True)
    yn = (ys - mu) * lax.rsqrt(var + EPS) * g_out[:, None, :].astype(F32)
    y = merge_heads(yn).astype(u.dtype) * jax.nn.silu(g)
    return y @ w_out


def lru_combine(c1, c2):
    a1, b1 = c1
    a2, b2 = c2
    return a1 * a2, a2 * b1 + b2


def rglru_mixer(u, w_in, conv_w, conv_b, w_gates, b_gates, lru_param, w_out):
    b, s, _ = u.shape
    gate_br, xb = jnp.split(u @ w_in, 2, axis=-1)
    xc = lax.conv_general_dilated(xb, conv_w[:, None, :].astype(xb.dtype), window_strides=(1,),
                                  padding=[(D_CONV - 1, 0)], dimension_numbers=('NWC', 'WIO', 'NWC'),
                                  feature_group_count=D_RNN) + conv_b
    gates = jnp.einsum('bsnc,nce->bsne', xc.reshape(b, s, D_BLOCKS, D_BW), w_gates) + b_gates
    r = jax.nn.sigmoid(gates[..., :D_BW].astype(F32)).reshape(b, s, D_RNN)
    i = jax.nn.sigmoid(gates[..., D_BW:].astype(F32)).reshape(b, s, D_RNN)
    log_a = -LRU_C * r * jax.nn.softplus(-lru_param.astype(F32))
    a = jnp.exp(log_a)
    bterm = jnp.sqrt(-jnp.expm1(2.0 * log_a)) * (i * xc.astype(F32))
    _, hs = lax.associative_scan(lru_combine, (a, bterm), axis=1)
    y = hs.astype(u.dtype) * jax.nn.gelu(gate_br)
    return y @ w_out


def swiglu(u, w_in, w_out):
    gt, up = jnp.split(u @ w_in, 2, axis=-1)
    return (jax.nn.silu(gt) * up) @ w_out


def setup_inputs(seed: int = 0) -> dict:
    key = jax.random.key(seed)
    ks = list(jax.random.split(key, 32))

    def rnd(shape, scale):
        return scale * jax.random.normal(ks.pop(), shape, F32)

    def gain(shape):
        return 1.0 + rnd(shape, 0.02)

    x = rnd((BATCH, SEQ, D_MODEL), 1.0)
    offs = jax.random.randint(ks.pop(), (BATCH, 1), 0, 1024, dtype=jnp.int32)
    positions = jnp.arange(SEQ, dtype=jnp.int32)[None, :] + offs
    norm_mix = gain((DEPTH, D_MODEL))
    norm_ffn = gain((DEPTH, D_MODEL))
    ffn_w_in = rnd((DEPTH, D_MODEL, 2 * D_FF), D_MODEL ** -0.5)
    ffn_w_out = rnd((DEPTH, D_FF, D_MODEL), OUT_SCALE * D_FF ** -0.5)
    a_w_in = rnd((N_LAYERS_A, D_MODEL, A_IN), D_MODEL ** -0.5)
    b_i = -1.0 + rnd((N_LAYERS_A, A_HEADS), 0.1)
    b_f = jnp.linspace(3.0, 6.0, A_HEADS, dtype=F32)[None, :] + rnd((N_LAYERS_A, A_HEADS), 0.1)
    a_b_gates = jnp.concatenate([b_i, b_f], axis=-1)
    a_g_out = gain((N_LAYERS_A, A_HEADS, A_DV))
    a_w_out = rnd((N_LAYERS_A, A_HEADS * A_DV, D_MODEL), OUT_SCALE * (A_HEADS * A_DV) ** -0.5)
    b_w_in = rnd((N_LAYERS_B, D_MODEL, B_IN), D_MODEL ** -0.5)
    b_g_q = gain((N_LAYERS_B, B_DH))
    b_g_k = gain((N_LAYERS_B, B_DH))
    b_w_out = rnd((N_LAYERS_B, B_HEADS * B_DH, D_MODEL), OUT_SCALE * (B_HEADS * B_DH) ** -0.5)
    c_w_in = rnd((N_LAYERS_C, D_MODEL, C_IN), D_MODEL ** -0.5)
    c_g_out = gain((N_LAYERS_C, C_HEADS, C_DV))
    c_w_out = rnd((N_LAYERS_C, C_HEADS * C_DV, D_MODEL), OUT_SCALE * (C_HEADS * C_DV) ** -0.5)
    d_w_in = rnd((N_LAYERS_D, D_MODEL, 2 * D_RNN), D_MODEL ** -0.5)
    d_conv_w = rnd((N_LAYERS_D, D_CONV, D_RNN), D_CONV ** -0.5)
    d_conv_b = rnd((N_LAYERS_D, D_RNN), 0.02)
    d_w_gates = rnd((N_LAYERS_D, D_BLOCKS, D_BW, 2 * D_BW), D_BW ** -0.5)
    d_b_gates = rnd((N_LAYERS_D, D_BLOCKS, 2 * D_BW), 0.1)
    a0 = jax.random.uniform(ks.pop(), (N_LAYERS_D, D_RNN), F32, minval=0.9, maxval=0.999)
    d_lru = jnp.log(a0) - jnp.log1p(-a0)
    d_w_out = rnd((N_LAYERS_D, D_RNN, D_MODEL), OUT_SCALE * D_RNN ** -0.5)
    return {'x': x, 'positions': positions, 'norm_mix': norm_mix, 'norm_ffn': norm_ffn,
            'ffn_w_in': ffn_w_in, 'ffn_w_out': ffn_w_out,
            'a_w_in': a_w_in, 'a_b_gates': a_b_gates, 'a_g_out': a_g_out, 'a_w_out': a_w_out,
            'b_w_in': b_w_in, 'b_g_q': b_g_q, 'b_g_k': b_g_k, 'b_w_out': b_w_out,
            'c_w_in': c_w_in, 'c_g_out': c_g_out, 'c_w_out': c_w_out,
            'd_w_in': d_w_in, 'd_conv_w': d_conv_w, 'd_conv_b': d_conv_b, 'd_w_gates': d_w_gates,
            'd_b_gates': d_b_gates, 'd_lru': d_lru, 'd_w_out': d_w_out}


def reference(x, positions, norm_mix, norm_ffn, ffn_w_in, ffn_w_out,
              a_w_in, a_b_gates, a_g_out, a_w_out,
              b_w_in, b_g_q, b_g_k, b_w_out,
              c_w_in, c_g_out, c_w_out,
              d_w_in, d_conv_w, d_conv_b, d_w_gates, d_b_gates, d_lru, d_w_out):
    h = x
    for layer in range(DEPTH):
        mixer, r = layer % N_MIXERS, layer // N_MIXERS
        u = rmsnorm(h, norm_mix[layer])
        if mixer == 0:
            y = mlstm_mixer(u, a_w_in[r], a_b_gates[r], a_g_out[r], a_w_out[r])
        elif mixer == 1:
            y = moba_mixer(u, positions, b_w_in[r], b_g_q[r], b_g_k[r], b_w_out[r])
        elif mixer == 2:
            y = retention_mixer(u, positions, c_w_in[r], c_g_out[r], c_w_out[r])
        else:
            y = rglru_mixer(u, d_w_in[r], d_conv_w[r], d_conv_b[r], d_w_gates[r], d_b_gates[r], d_lru[r], d_w_out[r])
        h = h + y
        h = h + swiglu(rmsnorm(h, norm_ffn[layer]), ffn_w_in[layer], ffn_w_out[layer])
    return h
```

```python
import functools
import math

import jax
import jax.numpy as jnp
from jax import lax
from jax.experimental import pallas as pl
from jax.experimental.pallas import tpu as pltpu

F32 = jnp.float32
BF16 = jnp.bfloat16
EPS = 1e-6
NEG_INF = float("-inf")
HIGHEST = lax.Precision.HIGHEST

LANES = 128
VMEM_LIMIT_BYTES = 56 * 1024 * 1024

A_HEADS, A_DQK, A_DV = 4, 128, 256
A_CHUNK = 128
B_HEADS, B_DH, B_BLOCK, B_TOPK, B_QBLOCK = 8, 128, 256, 3, 128
B_ROPE_THETA, B_ROPE_DIMS = 500000.0, 32
C_HEADS, C_DK, C_DV, C_CHUNK = 4, 256, 512, 128
C_ROPE_THETA = 10000.0
D_BLOCKS, D_BW, D_CONV, LRU_C = 4, 256, 4, 8.0


def _params(*semantics):
    return pltpu.CompilerParams(dimension_semantics=semantics, vmem_limit_bytes=VMEM_LIMIT_BYTES)


def _row_tile(t, want=512):
    return want if t % want == 0 else t


def _sigmoid(x):
    return 1.0 / (1.0 + jnp.exp(-x))


def _softplus(x):
    return jnp.maximum(x, 0.0) + jnp.log1p(jnp.exp(-jnp.abs(x)))


def _rms(x, g):
    return x * lax.rsqrt(jnp.mean(x * x, axis=-1, keepdims=True) + EPS) * g


def _dot(a, b):
    return jnp.dot(a, b, preferred_element_type=F32)


def _dot_nt(a, b):
    return lax.dot_general(a, b, (((1,), (1,)), ((), ())), preferred_element_type=F32)


def _norm_proj_kernel(x_ref, g_ref, *refs):
    n = len(refs) // 2
    u = _rms(x_ref[...], g_ref[...]).astype(BF16)
    for w_ref, o_ref in zip(refs[:n], refs[n:]):
        o_ref[...] = _dot(u, w_ref[...]).astype(o_ref.dtype)


def norm_proj(x2d, g, weights, out_dtypes):
    t, d = x2d.shape
    tm = _row_tile(t)
    in_specs = [pl.BlockSpec((tm, d), lambda i: (i, 0)), pl.BlockSpec((1, d), lambda i: (0, 0))]
    in_specs += [pl.BlockSpec(w.shape, lambda i: (0, 0)) for w in weights]
    out_specs = [pl.BlockSpec((tm, w.shape[1]), lambda i: (i, 0)) for w in weights]
    out_shape = [jax.ShapeDtypeStruct((t, w.shape[1]), dt) for w, dt in zip(weights, out_dtypes)]
    return pl.pallas_call(
        _norm_proj_kernel, grid=(t // tm,), in_specs=in_specs, out_specs=out_specs, out_shape=out_shape,
        compiler_params=_params("parallel"), name="norm_proj",
    )(x2d, g.reshape(1, d), *weights)


def _out_proj_kernel(y_ref, w_ref, r_ref, o_ref):
    o_ref[...] = r_ref[...] + _dot(y_ref[...], w_ref[...])


def out_proj(y2d, w, res2d):
    t, k = y2d.shape
    d = w.shape[1]
    tm = _row_tile(t)
    return pl.pallas_call(
        _out_proj_kernel, grid=(t // tm,),
        in_specs=[pl.BlockSpec((tm, k), lambda i: (i, 0)), pl.BlockSpec((k, d), lambda i: (0, 0)),
                  pl.BlockSpec((tm, d), lambda i: (i, 0))],
        out_specs=pl.BlockSpec((tm, d), lambda i: (i, 0)),
        out_shape=jax.ShapeDtypeStruct((t, d), F32),
        compiler_params=_params("parallel"), name="out_proj",
    )(y2d, w, res2d)


def _ffn_kernel(x_ref, g_ref, wg_ref, wu_ref, wo_ref, o_ref, *, n_chunks):
    x = x_ref[...]
    u = _rms(x, g_ref[...]).astype(BF16)
    fc = wg_ref.shape[1] // n_chunks
    acc = x
    for c in range(n_chunks):
        gt = _dot(u, wg_ref[:, c * fc:(c + 1) * fc])
        up = _dot(u, wu_ref[:, c * fc:(c + 1) * fc])
        act = (gt * _sigmoid(gt) * up).astype(BF16)
        acc = acc + _dot(act, wo_ref[c * fc:(c + 1) * fc, :])
    o_ref[...] = acc


def ffn(x2d, g, w_gate, w_up, w_out):
    t, d = x2d.shape
    dff = w_gate.shape[1]
    tm = _row_tile(t)
    n_chunks = 2 if dff % (2 * LANES) == 0 else 1
    return pl.pallas_call(
        functools.partial(_ffn_kernel, n_chunks=n_chunks), grid=(t // tm,),
        in_specs=[pl.BlockSpec((tm, d), lambda i: (i, 0)), pl.BlockSpec((1, d), lambda i: (0, 0)),
                  pl.BlockSpec((d, dff), lambda i: (0, 0)), pl.BlockSpec((d, dff), lambda i: (0, 0)),
                  pl.BlockSpec((dff, d), lambda i: (0, 0))],
        out_specs=pl.BlockSpec((tm, d), lambda i: (i, 0)),
        out_shape=jax.ShapeDtypeStruct((t, d), F32),
        compiler_params=_params("parallel"), name="ffn",
    )(x2d, g.reshape(1, d), w_gate, w_up, w_out)


def _mlstm_kernel(q_ref, k_ref, v_ref, o_ref, gc_ref, gr_ref, bc_ref, br_ref, gout_ref, y_ref,
                  ct_s, n_s, m_s):
    l = q_ref.shape[1]
    h, dqk, dv = A_HEADS, A_DQK, A_DV
    scale = dqk ** -0.5

    @pl.when(pl.program_id(1) == 0)
    def _():
        ct_s[...] = jnp.zeros_like(ct_s)
        n_s[...] = jnp.zeros_like(n_s)
        m_s[...] = jnp.zeros_like(m_s)

    gc = gc_ref[0] + bc_ref[...]
    gr = gr_ref[0, 0] + br_ref[...]
    lf_c = jnp.minimum(gc, 0.0) - jnp.log1p(jnp.exp(-jnp.abs(gc)))
    lf_r = jnp.minimum(gr, 0.0) - jnp.log1p(jnp.exp(-jnp.abs(gr)))
    row = lax.broadcasted_iota(jnp.int32, (l, l), 0)
    col = lax.broadcasted_iota(jnp.int32, (l, l), 1)
    causal = row >= col
    bcum_c = jnp.dot(causal.astype(F32), lf_c, precision=HIGHEST, preferred_element_type=F32)
    bcum_r = jnp.dot(lf_r, (row <= col).astype(F32), precision=HIGHEST, preferred_element_type=F32)

    q = q_ref[0]
    k = k_ref[0]
    v = v_ref[0]
    og = o_ref[0].astype(F32)
    for hh in range(h):
        bc = bcum_c[:, h + hh:h + hh + 1]
        br = bcum_r[h + hh:h + hh + 1, :]
        ic = gc[:, hh:hh + 1]
        ir = gr[hh:hh + 1, :]
        btot = bc[l - 1:l, :]
        m_st = m_s[hh][:, :1]
        qh = q[:, hh * dqk:(hh + 1) * dqk]
        kh = k[:, hh * dqk:(hh + 1) * dqk]
        vh = v[:, hh * dv:(hh + 1) * dv]
        ct = ct_s[hh]
        nst = n_s[hh]

        dmat = jnp.where(causal, bc - br + ir, NEG_INF)
        inter = bc + m_st
        m_t = jnp.maximum(inter, jnp.max(dmat, axis=1, keepdims=True))
        w_intra = jnp.exp(dmat - m_t) * scale
        w_inter = jnp.exp(inter - m_t) * scale
        qk = _dot_nt(qh, kh) * w_intra
        num = _dot(qk.astype(BF16), vh) + w_inter * _dot(qh, ct.astype(BF16))
        den = jnp.sum(qk, axis=1, keepdims=True) + w_inter * jnp.sum(qh.astype(F32) * nst, axis=1, keepdims=True)
        hout = num / jnp.maximum(jnp.abs(den), jnp.exp(-m_t))

        dec = btot - bc + ic
        m_new = jnp.maximum(btot + m_st, jnp.max(dec, axis=0, keepdims=True))
        ws = jnp.exp(dec - m_new)
        wc = jnp.exp(btot + m_st - m_new)
        kw = kh.astype(F32) * ws
        ct_s[hh] = wc * ct + _dot(kw.T.astype(BF16), vh)
        n_s[hh] = wc * nst + jnp.sum(kw, axis=0, keepdims=True)
        m_s[hh] = jnp.broadcast_to(m_new, (1, LANES))

        hn = _rms(hout, gout_ref[hh:hh + 1, :])
        y_ref[0, :, hh * dv:(hh + 1) * dv] = (hn * _sigmoid(og[:, hh * dv:(hh + 1) * dv])).astype(y_ref.dtype)


def mlstm_core(q, k, v, o, gates, b_gates, g_out):
    b, s, _ = q.shape
    h, l = A_HEADS, A_CHUNK if s % A_CHUNK == 0 else s
    nc = s // l
    gates_row = gates[..., :2 * h].reshape(b, nc, l, 2 * h).transpose(0, 1, 3, 2)
    bc = jnp.zeros((1, LANES), F32).at[0, :2 * h].set(b_gates)
    br = b_gates.reshape(2 * h, 1)
    sq, sv = h * A_DQK, h * A_DV
    return pl.pallas_call(
        _mlstm_kernel, grid=(b, nc),
        in_specs=[pl.BlockSpec((1, l, sq), lambda i, c: (i, c, 0)), pl.BlockSpec((1, l, sq), lambda i, c: (i, c, 0)),
                  pl.BlockSpec((1, l, sv), lambda i, c: (i, c, 0)), pl.BlockSpec((1, l, sv), lambda i, c: (i, c, 0)),
                  pl.BlockSpec((1, l, LANES), lambda i, c: (i, c, 0)),
                  pl.BlockSpec((1, 1, 2 * h, l), lambda i, c: (i, c, 0, 0)),
                  pl.BlockSpec((1, LANES), lambda i, c: (0, 0)), pl.BlockSpec((2 * h, 1), lambda i, c: (0, 0)),
                  pl.BlockSpec((h, A_DV), lambda i, c: (0, 0))],
        out_specs=pl.BlockSpec((1, l, sv), lambda i, c: (i, c, 0)),
        out_shape=jax.ShapeDtypeStruct((b, s, sv), BF16),
        scratch_shapes=[pltpu.VMEM((h, A_DQK, A_DV), F32), pltpu.VMEM((h, 1, A_DQK), F32),
                        pltpu.VMEM((h, 1, LANES), F32)],
        compiler_params=_params("parallel", "arbitrary"), name="mlstm_core",
    )(q, k, v, o, gates, gates_row, bc, br, g_out)


def mixer_a(h3d, g_norm, w_in, b_gates, g_out, w_out):
    b, s, d = h3d.shape
    hh = A_HEADS
    sq, sv = hh * A_DQK, hh * A_DV
    w = w_in.astype(BF16)
    w_gates = jnp.zeros((d, LANES), BF16).at[:, :2 * hh].set(w[:, 2 * sq + 2 * sv:])
    ws = [w[:, :sq], w[:, sq:2 * sq], w[:, 2 * sq:2 * sq + sv], w[:, 2 * sq + sv:2 * sq + 2 * sv], w_gates]
    x2d = h3d.reshape(b * s, d)
    q, k, v, o, gates = norm_proj(x2d, g_norm, ws, [BF16, BF16, BF16, BF16, F32])
    y = mlstm_core(q.reshape(b, s, sq), k.reshape(b, s, sq), v.reshape(b, s, sv), o.reshape(b, s, sv),
                   gates.reshape(b, s, LANES), b_gates, g_out)
    return out_proj(y.reshape(b * s, sv), w_out.astype(BF16), x2d).reshape(b, s, d)


def _moba_kernel(q_ref, k_ref, v_ref, pos_ref, inv_ref, gq_ref, gk_ref, o_ref, qn_s, kn_s, km_s, *, topk):
    s, dh = q_ref.shape[1], q_ref.shape[2]
    qb = o_ref.shape[1]
    nblk = s // B_BLOCK
    qi = pl.program_id(2)

    @pl.when(qi == 0)
    def _():
        ang = pos_ref[0] * inv_ref[...]
        cos, sin = jnp.cos(ang), jnp.sin(ang)
        lane = lax.broadcasted_iota(jnp.int32, (s, dh), 1)
        first = lane < B_ROPE_DIMS // 2
        sin = jnp.where(first, -sin, sin)

        def rot(x):
            partner = jnp.where(first, pltpu.roll(x, dh - B_ROPE_DIMS // 2, 1), pltpu.roll(x, B_ROPE_DIMS // 2, 1))
            return x * cos + partner * sin

        qn_s[...] = rot(_rms(q_ref[0], gq_ref[...])) * (dh ** -0.5)
        kn = rot(_rms(k_ref[0], gk_ref[...]))
        kn_s[...] = kn.astype(BF16)
        km_s[...] = jnp.zeros_like(km_s)
        for n in range(nblk):
            km_s[n:n + 1, :] = jnp.mean(kn[n * B_BLOCK:(n + 1) * B_BLOCK, :], axis=0, keepdims=True)

    j = qi // (B_BLOCK // qb)
    qf = qn_s[pl.ds(pl.multiple_of(qi * qb, qb), qb), :]
    gate = lax.dot_general(qf, km_s[...], (((1,), (1,)), ((), ())), precision=HIGHEST,
                           preferred_element_type=F32)
    blk = lax.broadcasted_iota(jnp.int32, gate.shape, 1)
    gate = jnp.where(blk < j, gate, NEG_INF)
    rank = jnp.zeros(gate.shape, F32)
    for m in range(nblk):
        gm = gate[:, m:m + 1]
        rank = rank + jnp.where((gm > gate) | ((gm == gate) & (blk > m)), 1.0, 0.0)
    sel = jnp.where((blk < j) & (rank < topk), 1.0, 0.0).astype(BF16)
    e_row = lax.broadcasted_iota(jnp.int32, (LANES, s), 0)
    e_key = lax.broadcasted_iota(jnp.int32, (LANES, s), 1)
    expand = jnp.where((e_key >= e_row * B_BLOCK) & (e_key < (e_row + 1) * B_BLOCK), 1.0, 0.0).astype(BF16)
    sel_keys = _dot(sel, expand)
    key = lax.broadcasted_iota(jnp.int32, (qb, s), 1)
    qpos = qi * qb + lax.broadcasted_iota(jnp.int32, (qb, s), 0)
    mask = (sel_keys > 0.5) | ((key >= j * B_BLOCK) & (key <= qpos))
    sc = jnp.where(mask, _dot_nt(qf.astype(BF16), kn_s[...]), NEG_INF)
    p = jnp.exp(sc - jnp.max(sc, axis=1, keepdims=True))
    out = _dot(p.astype(BF16), v_ref[0]) / jnp.sum(p, axis=1, keepdims=True)
    o_ref[0] = out.astype(o_ref.dtype)


def moba_core(q, k, v, positions, g_q, g_k):
    b, s, hd = q.shape
    h, dh, qb = B_HEADS, B_DH, B_QBLOCK
    assert s % B_BLOCK == 0 and s // B_BLOCK <= LANES
    topk = min(B_TOPK, s // B_BLOCK - 1)
    half = B_ROPE_DIMS // 2
    inv = B_ROPE_THETA ** (-jnp.arange(half, dtype=F32) * (2.0 / B_ROPE_DIMS))
    inv_row = jnp.zeros((1, dh), F32).at[0, :half].set(inv).at[0, half:2 * half].set(inv)
    pos = positions.astype(F32).reshape(b, s, 1)
    head_spec = pl.BlockSpec((1, s, dh), lambda i, hh, c: (i, 0, hh))
    const = lambda i, hh, c: (0, 0)
    return pl.pallas_call(
        functools.partial(_moba_kernel, topk=topk), grid=(b, h, s // qb),
        in_specs=[head_spec, head_spec, head_spec, pl.BlockSpec((1, s, 1), lambda i, hh, c: (i, 0, 0)),
                  pl.BlockSpec((1, dh), const), pl.BlockSpec((1, dh), const), pl.BlockSpec((1, dh), const)],
        out_specs=pl.BlockSpec((1, qb, dh), lambda i, hh, c: (i, c, hh)),
        out_shape=jax.ShapeDtypeStruct((b, s, hd), BF16),
        scratch_shapes=[pltpu.VMEM((s, dh), F32), pltpu.VMEM((s, dh), BF16), pltpu.VMEM((LANES, dh), F32)],
        compiler_params=_params("parallel", "parallel", "arbitrary"), name="moba_core",
    )(q, k, v, pos, inv_row, g_q.reshape(1, dh), g_k.reshape(1, dh))


def mixer_b(h3d, positions, g_norm, w_in, g_q, g_k, w_out):
    b, s, d = h3d.shape
    hd = B_HEADS * B_DH
    w = w_in.astype(BF16)
    x2d = h3d.reshape(b * s, d)
    q, k, v = norm_proj(x2d, g_norm, [w[:, :hd], w[:, hd:2 * hd], w[:, 2 * hd:]], [F32, F32, BF16])
    y = moba_core(q.reshape(b, s, hd), k.reshape(b, s, hd), v.reshape(b, s, hd), positions, g_q, g_k)
    return out_proj(y.reshape(b * s, hd), w_out.astype(BF16), x2d).reshape(b, s, d)


def _retention_kernel(q_ref, k_ref, v_ref, g_ref, pos_ref, inv_ref, gout_ref, y_ref, r_s):
    l = q_ref.shape[1]
    h, dk, dv = C_HEADS, C_DK, C_DV
    half = dk // 2

    @pl.when(pl.program_id(1) == 0)
    def _():
        r_s[...] = jnp.zeros_like(r_s)

    ang = pos_ref[0] * inv_ref[...]
    cos, sin = jnp.cos(ang), jnp.sin(ang)
    row = lax.broadcasted_iota(jnp.int32, (l, l), 0)
    col = lax.broadcasted_iota(jnp.int32, (l, l), 1)
    diff = (row - col).astype(F32)
    t_col = lax.broadcasted_iota(jnp.int32, (l, 1), 0).astype(F32)

    def rot(x):
        x1, x2 = x[:, :half], x[:, half:]
        return jnp.concatenate([x1 * cos - x2 * sin, x2 * cos + x1 * sin], axis=1)

    q = q_ref[0].astype(F32)
    k = k_ref[0].astype(F32)
    v = v_ref[0]
    g = g_ref[0].astype(F32)
    for hh in range(h):
        log_g = math.log1p(-(2.0 ** (-5.0 - hh)))
        intra = jnp.where(diff >= 0, jnp.exp(jnp.maximum(diff, 0.0) * log_g), 0.0)
        xi = jnp.exp((t_col + 1.0) * log_g)
        zeta = jnp.exp((l - 1.0 - t_col) * log_g)
        qh = rot(q[:, hh * dk:(hh + 1) * dk]).astype(BF16)
        kf = rot(k[:, hh * dk:(hh + 1) * dk]) * (dk ** -0.5)
        vh = v[:, hh * dv:(hh + 1) * dv]
        r_st = r_s[hh]
        inner = _dot((_dot_nt(qh, kf.astype(BF16)) * intra).astype(BF16), vh)
        cross = _dot(qh, r_st.astype(BF16)) * xi
        r_s[hh] = math.exp(l * log_g) * r_st + _dot((kf * zeta).T.astype(BF16), vh)
        ys = inner + cross
        mu = jnp.mean(ys, axis=1, keepdims=True)
        yc = ys - mu
        yn = yc * lax.rsqrt(jnp.mean(yc * yc, axis=1, keepdims=True) + EPS) * gout_ref[hh:hh + 1, :]
        gh = g[:, hh * dv:(hh + 1) * dv]
        y_ref[0, :, hh * dv:(hh + 1) * dv] = (yn * (gh * _sigmoid(gh))).astype(y_ref.dtype)


def retention_core(q, k, v, g, positions, g_out):
    b, s, _ = q.shape
    h, l = C_HEADS, C_CHUNK if s % C_CHUNK == 0 else s
    sk, sv = h * C_DK, h * C_DV
    half = C_DK // 2
    inv = (C_ROPE_THETA ** (-jnp.arange(half, dtype=F32) * (2.0 / C_DK))).reshape(1, half)
    pos = positions.astype(F32).reshape(b, s, 1)
    return pl.pallas_call(
        _retention_kernel, grid=(b, s // l),
        in_specs=[pl.BlockSpec((1, l, sk), lambda i, c: (i, c, 0)), pl.BlockSpec((1, l, sk), lambda i, c: (i, c, 0)),
                  pl.BlockSpec((1, l, sv), lambda i, c: (i, c, 0)), pl.BlockSpec((1, l, sv), lambda i, c: (i, c, 0)),
                  pl.BlockSpec((1, l, 1), lambda i, c: (i, c, 0)), pl.BlockSpec((1, half), lambda i, c: (0, 0)),
                  pl.BlockSpec((h, C_DV), lambda i, c: (0, 0))],
        out_specs=pl.BlockSpec((1, l, sv), lambda i, c: (i, c, 0)),
        out_shape=jax.ShapeDtypeStruct((b, s, sv), BF16),
        scratch_shapes=[pltpu.VMEM((h, C_DK, C_DV), F32)],
        compiler_params=_params("parallel", "arbitrary"), name="retention_core",
    )(q, k, v, g, pos, inv, g_out)


def mixer_c(h3d, positions, g_norm, w_in, g_out, w_out):
    b, s, d = h3d.shape
    sk, sv = C_HEADS * C_DK, C_HEADS * C_DV
    w = w_in.astype(BF16)
    x2d = h3d.reshape(b * s, d)
    ws = [w[:, :sk], w[:, sk:2 * sk], w[:, 2 * sk:2 * sk + sv], w[:, 2 * sk + sv:]]
    q, k, v, g = norm_proj(x2d, g_norm, ws, [BF16, BF16, BF16, BF16])
    y = retention_core(q.reshape(b, s, sk), k.reshape(b, s, sk), v.reshape(b, s, sv), g.reshape(b, s, sv),
                       positions, g_out)
    return out_proj(y.reshape(b * s, sv), w_out.astype(BF16), x2d).reshape(b, s, d)


def _rglru_kernel(xb_ref, gb_ref, cw_ref, cb_ref, wg_ref, bg_ref, lru_ref, y_ref):
    s, bw = xb_ref.shape[1], xb_ref.shape[2]
    x = xb_ref[0]
    t = lax.broadcasted_iota(jnp.int32, (s, bw), 0)
    xc = x * cw_ref[D_CONV - 1:D_CONV, :] + cb_ref[...]
    for d in range(1, D_CONV):
        xc = xc + jnp.where(t >= d, pltpu.roll(x, d, 0), 0.0) * cw_ref[D_CONV - 1 - d:D_CONV - d, :]
    gates = _dot(xc.astype(BF16), wg_ref[0]) + bg_ref[0]
    r = _sigmoid(gates[:, :bw])
    i = _sigmoid(gates[:, bw:])
    log_a = -LRU_C * r * _softplus(-lru_ref[...])
    a = jnp.exp(log_a)
    bt = jnp.sqrt(jnp.tanh(-log_a) * (1.0 + a * a)) * (i * xc)
    d = 1
    while d < s:
        keep = t >= d
        a_sh = jnp.where(keep, pltpu.roll(a, d, 0), 1.0)
        b_sh = jnp.where(keep, pltpu.roll(bt, d, 0), 0.0)
        bt = a * b_sh + bt
        a = a * a_sh
        d *= 2
    y_ref[0] = (bt * jax.nn.gelu(gb_ref[0].astype(F32))).astype(y_ref.dtype)


def rglru_core(gate_br, xb, conv_w, conv_b, w_gates, b_gates, lru):
    b, s, dr = xb.shape
    bw = D_BW
    chan = lambda i, n: (i, 0, n)
    return pl.pallas_call(
        _rglru_kernel, grid=(b, dr // bw),
        in_specs=[pl.BlockSpec((1, s, bw), chan), pl.BlockSpec((1, s, bw), chan),
                  pl.BlockSpec((D_CONV, bw), lambda i, n: (0, n)), pl.BlockSpec((1, bw), lambda i, n: (0, n)),
                  pl.BlockSpec((1, bw, 2 * bw), lambda i, n: (n, 0, 0)),
                  pl.BlockSpec((1, 1, 2 * bw), lambda i, n: (n, 0, 0)),
                  pl.BlockSpec((1, bw), lambda i, n: (0, n))],
        out_specs=pl.BlockSpec((1, s, bw), chan),
        out_shape=jax.ShapeDtypeStruct((b, s, dr), BF16),
        compiler_params=_params("parallel", "parallel"), name="rglru_core",
    )(xb, gate_br, conv_w, conv_b.reshape(1, dr), w_gates.astype(BF16), b_gates.reshape(D_BLOCKS, 1, 2 * bw),
      lru.reshape(1, dr))


def mixer_d(h3d, g_norm, w_in, conv_w, conv_b, w_gates, b_gates, lru, w_out):
    b, s, d = h3d.shape
    dr = D_BLOCKS * D_BW
    w = w_in.astype(BF16)
    x2d = h3d.reshape(b * s, d)
    gate_br, xb = norm_proj(x2d, g_norm, [w[:, :dr], w[:, dr:]], [BF16, F32])
    y = rglru_core(gate_br.reshape(b, s, dr), xb.reshape(b, s, dr), conv_w, conv_b, w_gates, b_gates, lru)
    return out_proj(y.reshape(b * s, dr), w_out.astype(BF16), x2d).reshape(b, s, d)


def ffn_layer(h3d, g_norm, w_in, w_out):
    b, s, d = h3d.shape
    dff = w_out.shape[0]
    w = w_in.astype(BF16)
    return ffn(h3d.reshape(b * s, d), g_norm, w[:, :dff], w[:, dff:], w_out.astype(BF16)).reshape(b, s, d)


def kernel(x, positions, norm_mix, norm_ffn, ffn_w_in, ffn_w_out, a_w_in, a_b_gates, a_g_out, a_w_out, b_w_in, b_g_q, b_g_k, b_w_out, c_w_in, c_g_out, c_w_out, d_w_in, d_conv_w, d_conv_b, d_w_gates, d_b_gates, d_lru, d_w_out):
    h = x
    depth = norm_mix.shape[0]
    for layer in range(depth):
        mixer, r = layer % 4, layer // 4
        if mixer == 0:
            h = mixer_a(h, norm_mix[layer], a_w_in[r], a_b_gates[r], a_g_out[r], a_w_out[r])
        elif mixer == 1:
            h = mixer_b(h, positions, norm_mix[layer], b_w_in[r], b_g_q[r], b_g_k[r], b_w_out[r])
        elif mixer == 2:
            h = mixer_c(h, positions, norm_mix[layer], c_w_in[r], c_g_out[r], c_w_out[r])
        else:
            h = mixer_d(h, norm_mix[layer], d_w_in[r], d_conv_w[r], d_conv_b[r], d_w_gates[r], d_b_gates[r],
                        d_lru[r], d_w_out[r])
        h = ffn_layer(h, norm_ffn[layer], ffn_w_in[layer], ffn_w_out[layer])
    return h
```

```python
import functools
import math

import jax
import jax.numpy as jnp
from jax import lax
from jax.experimental import pallas as pl
from jax.experimental.pallas import tpu as pltpu

F32 = jnp.float32
BF16 = jnp.bfloat16
EPS = 1e-6
NEG_INF = float("-inf")
HIGHEST = lax.Precision.HIGHEST

LANES = 128
VMEM_LIMIT_BYTES = 56 * 1024 * 1024

A_HEADS, A_DQK, A_DV = 4, 128, 256
A_CHUNK = 128
B_HEADS, B_DH, B_BLOCK, B_TOPK, B_QBLOCK = 8, 128, 256, 3, 128
B_ROPE_THETA, B_ROPE_DIMS = 500000.0, 32
C_HEADS, C_DK, C_DV, C_CHUNK = 4, 256, 512, 128
C_ROPE_THETA = 10000.0
D_BLOCKS, D_BW, D_CONV, LRU_C = 4, 256, 4, 8.0


def _params(*semantics):
    return pltpu.CompilerParams(dimension_semantics=semantics, vmem_limit_bytes=VMEM_LIMIT_BYTES)


def _row_tile(t, want=512):
    return want if t % want == 0 else t


def _sigmoid(x):
    return 1.0 / (1.0 + jnp.exp(-x))


def _softplus(x):
    return jnp.maximum(x, 0.0) + jnp.log1p(jnp.exp(-jnp.abs(x)))


def _rms(x, g):
    return x * lax.rsqrt(jnp.mean(x * x, axis=-1, keepdims=True) + EPS) * g


def _dot(a, b):
    return jnp.dot(a, b, preferred_element_type=F32)


def _dot_nt(a, b):
    return lax.dot_general(a, b, (((1,), (1,)), ((), ())), preferred_element_type=F32)


def _norm_proj_kernel(x_ref, g_ref, *refs):
    n = len(refs) // 2
    u = _rms(x_ref[...], g_ref[...]).astype(BF16)
    for w_ref, o_ref in zip(refs[:n], refs[n:]):
        o_ref[...] = _dot(u, w_ref[...]).astype(o_ref.dtype)


def norm_proj(x2d, g, weights, out_dtypes):
    t, d = x2d.shape
    tm = _row_tile(t)
    in_specs = [pl.BlockSpec((tm, d), lambda i: (i, 0)), pl.BlockSpec((1, d), lambda i: (0, 0))]
    in_specs += [pl.BlockSpec(w.shape, lambda i: (0, 0)) for w in weights]
    out_specs = [pl.BlockSpec((tm, w.shape[1]), lambda i: (i, 0)) for w in weights]
    out_shape = [jax.ShapeDtypeStruct((t, w.shape[1]), dt) for w, dt in zip(weights, out_dtypes)]
    return pl.pallas_call(
        _norm_proj_kernel, grid=(t // tm,), in_specs=in_specs, out_specs=out_specs, out_shape=out_shape,
        compiler_params=_params("parallel"), name="norm_proj",
    )(x2d, g.reshape(1, d), *weights)


def _out_proj_kernel(y_ref, w_ref, r_ref, o_ref):
    o_ref[...] = r_ref[...] + _dot(y_ref[...], w_ref[...])


def out_proj(y2d, w, res2d):
    t, k = y2d.shape
    d = w.shape[1]
    tm = _row_tile(t)
    return pl.pallas_call(
        _out_proj_kernel, grid=(t // tm,),
        in_specs=[pl.BlockSpec((tm, k), lambda i: (i, 0)), pl.BlockSpec((k, d), lambda i: (0, 0)),
                  pl.BlockSpec((tm, d), lambda i: (i, 0))],
        out_specs=pl.BlockSpec((tm, d), lambda i: (i, 0)),
        out_shape=jax.ShapeDtypeStruct((t, d), F32),
        compiler_params=_params("parallel"), name="out_proj",
    )(y2d, w, res2d)


def _ffn_kernel(x_ref, g_ref, wg_ref, wu_ref, wo_ref, o_ref, *, n_chunks):
    x = x_ref[...]
    u = _rms(x, g_ref[...]).astype(BF16)
    fc = wg_ref.shape[1] // n_chunks
    acc = x
    for c in range(n_chunks):
        gt = _dot(u, wg_ref[:, c * fc:(c + 1) * fc])
        up = _dot(u, wu_ref[:, c * fc:(c + 1) * fc])
        act = (gt * _sigmoid(gt) * up).astype(BF16)
        acc = acc + _dot(act, wo_ref[c * fc:(c + 1) * fc, :])
    o_ref[...] = acc


def ffn(x2d, g, w_gate, w_up, w_out):
    t, d = x2d.shape
    dff = w_gate.shape[1]
    tm = _row_tile(t)
    n_chunks = 2 if dff % (2 * LANES) == 0 else 1
    return pl.pallas_call(
        functools.partial(_ffn_kernel, n_chunks=n_chunks), grid=(t // tm,),
        in_specs=[pl.BlockSpec((tm, d), lambda i: (i, 0)), pl.BlockSpec((1, d), lambda i: (0, 0)),
                  pl.BlockSpec((d, dff), lambda i: (0, 0)), pl.BlockSpec((d, dff), lambda i: (0, 0)),
                  pl.BlockSpec((dff, d), lambda i: (0, 0))],
        out_specs=pl.BlockSpec((tm, d), lambda i: (i, 0)),
        out_shape=jax.ShapeDtypeStruct((t, d), F32),
        compiler_params=_params("parallel"), name="ffn",
    )(x2d, g.reshape(1, d), w_gate, w_up, w_out)


def _mlstm_kernel(q_ref, k_ref, v_ref, o_ref, gc_ref, gr_ref, bc_ref, br_ref, gout_ref, y_ref,
                  ct_s, n_s, m_s):
    l = q_ref.shape[1]
    h, dqk, dv = A_HEADS, A_DQK, A_DV
    scale = dqk ** -0.5

    @pl.when(pl.program_id(1) == 0)
    def _():
        ct_s[...] = jnp.zeros_like(ct_s)
        n_s[...] = jnp.zeros_like(n_s)
        m_s[...] = jnp.zeros_like(m_s)

    gc = gc_ref[0] + bc_ref[...]
    gr = gr_ref[0, 0] + br_ref[...]
    lf_c = jnp.minimum(gc, 0.0) - jnp.log1p(jnp.exp(-jnp.abs(gc)))
    lf_r = jnp.minimum(gr, 0.0) - jnp.log1p(jnp.exp(-jnp.abs(gr)))
    row = lax.broadcasted_iota(jnp.int32, (l, l), 0)
    col = lax.broadcasted_iota(jnp.int32, (l, l), 1)
    causal = row >= col
    bcum_c = jnp.dot(causal.astype(F32), lf_c, precision=HIGHEST, preferred_element_type=F32)
    bcum_r = jnp.dot(lf_r, (row <= col).astype(F32), precision=HIGHEST, preferred_element_type=F32)

    q = q_ref[0]
    k = k_ref[0]
    v = v_ref[0]
    og = o_ref[0].astype(F32)
    for hh in range(h):
        bc = bcum_c[:, h + hh:h + hh + 1]
        br = bcum_r[h + hh:h + hh + 1, :]
        ic = gc[:, hh:hh + 1]
        ir = gr[hh:hh + 1, :]
        btot = bc[l - 1:l, :]
        m_st = m_s[hh][:, :1]
        qh = q[:, hh * dqk:(hh + 1) * dqk]
        kh = k[:, hh * dqk:(hh + 1) * dqk]
        vh = v[:, hh * dv:(hh + 1) * dv]
        ct = ct_s[hh]
        nst = n_s[hh]

        dmat = jnp.where(causal, bc - br + ir, NEG_INF)
        inter = bc + m_st
        m_t = jnp.maximum(inter, jnp.max(dmat, axis=1, keepdims=True))
        w_intra = jnp.exp(dmat - m_t) * scale
        w_inter = jnp.exp(inter - m_t) * scale
        qk = _dot_nt(qh, kh) * w_intra
        num = _dot(qk.astype(BF16), vh) + w_inter * _dot(qh, ct.astype(BF16))
        den = jnp.sum(qk, axis=1, keepdims=True) + w_inter * jnp.sum(qh.astype(F32) * nst, axis=1, keepdims=True)
        hout = num / jnp.maximum(jnp.abs(den), jnp.exp(-m_t))

        dec = btot - bc + ic
        m_new = jnp.maximum(btot + m_st, jnp.max(dec, axis=0, keepdims=True))
        ws = jnp.exp(dec - m_new)
        wc = jnp.exp(btot + m_st - m_new)
        kw = kh.astype(F32) * ws
        ct_s[hh] = wc * ct + _dot(kw.T.astype(BF16), vh)
        n_s[hh] = wc * nst + jnp.sum(kw, axis=0, keepdims=True)
        m_s[hh] = jnp.broadcast_to(m_new, (1, LANES))

        hn = _rms(hout, gout_ref[hh:hh + 1, :])
        y_ref[0, :, hh * dv:(hh + 1) * dv] = (hn * _sigmoid(og[:, hh * dv:(hh + 1) * dv])).astype(y_ref.dtype)


def mlstm_core(q, k, v, o, gates, b_gates, g_out):
    b, s, _ = q.shape
    h, l = A_HEADS, A_CHUNK if s % A_CHUNK == 0 else s
    nc = s // l
    gates_row = gates[..., :2 * h].reshape(b, nc, l, 2 * h).transpose(0, 1, 3, 2)
    bc = jnp.zeros((1, LANES), F32).at[0, :2 * h].set(b_gates)
    br = b_gates.reshape(2 * h, 1)
    sq, sv = h * A_DQK, h * A_DV
    return pl.pallas_call(
        _mlstm_kernel, grid=(b, nc),
        in_specs=[pl.BlockSpec((1, l, sq), lambda i, c: (i, c, 0)), pl.BlockSpec((1, l, sq), lambda i, c: (i, c, 0)),
                  pl.BlockSpec((1, l, sv), lambda i, c: (i, c, 0)), pl.BlockSpec((1, l, sv), lambda i, c: (i, c, 0)),
                  pl.BlockSpec((1, l, LANES), lambda i, c: (i, c, 0)),
                  pl.BlockSpec((1, 1, 2 * h, l), lambda i, c: (i, c, 0, 0)),
                  pl.BlockSpec((1, LANES), lambda i, c: (0, 0)), pl.BlockSpec((2 * h, 1), lambda i, c: (0, 0)),
                  pl.BlockSpec((h, A_DV), lambda i, c: (0, 0))],
        out_specs=pl.BlockSpec((1, l, sv), lambda i, c: (i, c, 0)),
        out_shape=jax.ShapeDtypeStruct((b, s, sv), BF16),
        scratch_shapes=[pltpu.VMEM((h, A_DQK, A_DV), F32), pltpu.VMEM((h, 1, A_DQK), F32),
                        pltpu.VMEM((h, 1, LANES), F32)],
        compiler_params=_params("parallel", "arbitrary"), name="mlstm_core",
    )(q, k, v, o, gates, gates_row, bc, br, g_out)


def mixer_a(h3d, g_norm, w_in, b_gates, g_out, w_out):
    b, s, d = h3d.shape
    hh = A_HEADS
    sq, sv = hh * A_DQK, hh * A_DV
    w = w_in.astype(BF16)
    w_gates = jnp.zeros((d, LANES), BF16).at[:, :2 * hh].set(w[:, 2 * sq + 2 * sv:])
    ws = [w[:, :sq], w[:, sq:2 * sq], w[:, 2 * sq:2 * sq + sv], w[:, 2 * sq + sv:2 * sq + 2 * sv], w_gates]
    x2d = h3d.reshape(b * s, d)
    q, k, v, o, gates = norm_proj(x2d, g_norm, ws, [BF16, BF16, BF16, BF16, F32])
    y = mlstm_core(q.reshape(b, s, sq), k.reshape(b, s, sq), v.reshape(b, s, sv), o.reshape(b, s, sv),
                   gates.reshape(b, s, LANES), b_gates, g_out)
    return out_proj(y.reshape(b * s, sv), w_out.astype(BF16), x2d).reshape(b, s, d)


def _moba_kernel(q_ref, k_ref, v_ref, pos_ref, inv_ref, gq_ref, gk_ref, o_ref, cos_s, sin_s, km_s, *, topk):
    s, dh = q_ref.shape[1], q_ref.shape[2]
    nblk = s // B_BLOCK
    half = B_ROPE_DIMS // 2
    first = lax.broadcasted_iota(jnp.int32, (s, dh), 1) < half

    @pl.when(pl.program_id(1) == 0)
    def _():
        ang = pos_ref[0] * inv_ref[...]
        cos_s[...] = jnp.cos(ang)
        sin_s[...] = jnp.where(first, -jnp.sin(ang), jnp.sin(ang))

    def rot(x):
        partner = jnp.where(first, pltpu.roll(x, dh - half, 1), pltpu.roll(x, half, 1))
        return x * cos_s[...] + partner * sin_s[...]

    qn = rot(_rms(q_ref[0], gq_ref[...])) * (dh ** -0.5)
    kn = rot(_rms(k_ref[0], gk_ref[...]))
    qb16 = qn.astype(BF16)
    kb16 = kn.astype(BF16)
    km_s[...] = jnp.zeros_like(km_s)
    for n in range(nblk):
        km_s[n:n + 1, :] = jnp.mean(kn[n * B_BLOCK:(n + 1) * B_BLOCK, :], axis=0, keepdims=True)

    blk = lax.broadcasted_iota(jnp.int32, (B_BLOCK, LANES), 1)
    tri = (lax.broadcasted_iota(jnp.int32, (B_BLOCK, B_BLOCK), 0)
           >= lax.broadcasted_iota(jnp.int32, (B_BLOCK, B_BLOCK), 1))
    for j in range(nblk):
        rows = slice(j * B_BLOCK, (j + 1) * B_BLOCK)
        qj = qb16[rows]
        select = j > topk
        if select:
            gate = lax.dot_general(qn[rows], km_s[...], (((1,), (1,)), ((), ())), precision=HIGHEST,
                                   preferred_element_type=F32)
            gate = jnp.where(blk < j, gate, NEG_INF)
            rank = jnp.zeros(gate.shape, F32)
            for m in range(j):
                gm = gate[:, m:m + 1]
                rank = rank + jnp.where((gm > gate) | ((gm == gate) & (blk > m)), 1.0, 0.0)
            bias = jnp.where(rank < topk, 0.0, NEG_INF)
        parts = []
        for n in range(j):
            sc = _dot_nt(qj, kb16[n * B_BLOCK:(n + 1) * B_BLOCK])
            parts.append(sc + bias[:, n:n + 1] if select else sc)
        parts.append(jnp.where(tri, _dot_nt(qj, kb16[rows]), NEG_INF))
        sc = jnp.concatenate(parts, axis=1) if j else parts[0]
        p = jnp.exp(sc - jnp.max(sc, axis=1, keepdims=True))
        out = _dot(p.astype(BF16), v_ref[0, :(j + 1) * B_BLOCK, :]) / jnp.sum(p, axis=1, keepdims=True)
        o_ref[0, rows, :] = out.astype(o_ref.dtype)


def moba_core(q, k, v, positions, g_q, g_k):
    b, s, hd = q.shape
    h, dh = B_HEADS, B_DH
    assert s % B_BLOCK == 0 and s // B_BLOCK <= LANES
    topk = min(B_TOPK, s // B_BLOCK - 1)
    half = B_ROPE_DIMS // 2
    inv = B_ROPE_THETA ** (-jnp.arange(half, dtype=F32) * (2.0 / B_ROPE_DIMS))
    inv_row = jnp.zeros((1, dh), F32).at[0, :half].set(inv).at[0, half:2 * half].set(inv)
    pos = positions.astype(F32).reshape(b, s, 1)
    head_spec = pl.BlockSpec((1, s, dh), lambda i, hh: (i, 0, hh))
    const = lambda i, hh: (0, 0)
    return pl.pallas_call(
        functools.partial(_moba_kernel, topk=topk), grid=(b, h),
        in_specs=[head_spec, head_spec, head_spec, pl.BlockSpec((1, s, 1), lambda i, hh: (i, 0, 0)),
                  pl.BlockSpec((1, dh), const), pl.BlockSpec((1, dh), const), pl.BlockSpec((1, dh), const)],
        out_specs=head_spec,
        out_shape=jax.ShapeDtypeStruct((b, s, hd), BF16),
        scratch_shapes=[pltpu.VMEM((s, dh), F32), pltpu.VMEM((s, dh), F32), pltpu.VMEM((LANES, dh), F32)],
        compiler_params=_params("parallel", "arbitrary"), name="moba_core",
    )(q, k, v, pos, inv_row, g_q.reshape(1, dh), g_k.reshape(1, dh))


def mixer_b(h3d, positions, g_norm, w_in, g_q, g_k, w_out):
    b, s, d = h3d.shape
    hd = B_HEADS * B_DH
    w = w_in.astype(BF16)
    x2d = h3d.reshape(b * s, d)
    q, k, v = norm_proj(x2d, g_norm, [w[:, :hd], w[:, hd:2 * hd], w[:, 2 * hd:]], [F32, F32, BF16])
    y = moba_core(q.reshape(b, s, hd), k.reshape(b, s, hd), v.reshape(b, s, hd), positions, g_q, g_k)
    return out_proj(y.reshape(b * s, hd), w_out.astype(BF16), x2d).reshape(b, s, d)


def _retention_kernel(q_ref, k_ref, v_ref, g_ref, pos_ref, inv_ref, gout_ref, y_ref, r_s):
    l = q_ref.shape[1]
    h, dk, dv = C_HEADS, C_DK, C_DV
    half = dk // 2

    @pl.when(pl.program_id(1) == 0)
    def _():
        r_s[...] = jnp.zeros_like(r_s)

    ang = pos_ref[0] * inv_ref[...]
    cos, sin = jnp.cos(ang), jnp.sin(ang)
    row = lax.broadcasted_iota(jnp.int32, (l, l), 0)
    col = lax.broadcasted_iota(jnp.int32, (l, l), 1)
    diff = (row - col).astype(F32)
    t_col = lax.broadcasted_iota(jnp.int32, (l, 1), 0).astype(F32)

    def rot(x):
        x1, x2 = x[:, :half], x[:, half:]
        return jnp.concatenate([x1 * cos - x2 * sin, x2 * cos + x1 * sin], axis=1)

    q = q_ref[0].astype(F32)
    k = k_ref[0].astype(F32)
    v = v_ref[0]
    g = g_ref[0].astype(F32)
    for hh in range(h):
        log_g = math.log1p(-(2.0 ** (-5.0 - hh)))
        intra = jnp.where(diff >= 0, jnp.exp(jnp.maximum(diff, 0.0) * log_g), 0.0)
        xi = jnp.exp((t_col + 1.0) * log_g)
        zeta = jnp.exp((l - 1.0 - t_col) * log_g)
        qh = rot(q[:, hh * dk:(hh + 1) * dk]).astype(BF16)
        kf = rot(k[:, hh * dk:(hh + 1) * dk]) * (dk ** -0.5)
        vh = v[:, hh * dv:(hh + 1) * dv]
        r_st = r_s[hh]
        inner = _dot((_dot_nt(qh, kf.astype(BF16)) * intra).astype(BF16), vh)
        cross = _dot(qh, r_st.astype(BF16)) * xi
        r_s[hh] = math.exp(l * log_g) * r_st + _dot((kf * zeta).T.astype(BF16), vh)
        ys = inner + cross
        mu = jnp.mean(ys, axis=1, keepdims=True)
        yc = ys - mu
        yn = yc * lax.rsqrt(jnp.mean(yc * yc, axis=1, keepdims=True) + EPS) * gout_ref[hh:hh + 1, :]
        gh = g[:, hh * dv:(hh + 1) * dv]
        y_ref[0, :, hh * dv:(hh + 1) * dv] = (yn * (gh * _sigmoid(gh))).astype(y_ref.dtype)


def retention_core(q, k, v, g, positions, g_out):
    b, s, _ = q.shape
    h, l = C_HEADS, C_CHUNK if s % C_CHUNK == 0 else s
    sk, sv = h * C_DK, h * C_DV
    half = C_DK // 2
    inv = (C_ROPE_THETA ** (-jnp.arange(half, dtype=F32) * (2.0 / C_DK))).reshape(1, half)
    pos = positions.astype(F32).reshape(b, s, 1)
    return pl.pallas_call(
        _retention_kernel, grid=(b, s // l),
        in_specs=[pl.BlockSpec((1, l, sk), lambda i, c: (i, c, 0)), pl.BlockSpec((1, l, sk), lambda i, c: (i, c, 0)),
                  pl.BlockSpec((1, l, sv), lambda i, c: (i, c, 0)), pl.BlockSpec((1, l, sv), lambda i, c: (i, c, 0)),
                  pl.BlockSpec((1, l, 1), lambda i, c: (i, c, 0)), pl.BlockSpec((1, half), lambda i, c: (0, 0)),
                  pl.BlockSpec((h, C_DV), lambda i, c: (0, 0))],
        out_specs=pl.BlockSpec((1, l, sv), lambda i, c: (i, c, 0)),
        out_shape=jax.ShapeDtypeStruct((b, s, sv), BF16),
        scratch_shapes=[pltpu.VMEM((h, C_DK, C_DV), F32)],
        compiler_params=_params("parallel", "arbitrary"), name="retention_core",
    )(q, k, v, g, pos, inv, g_out)


def mixer_c(h3d, positions, g_norm, w_in, g_out, w_out):
    b, s, d = h3d.shape
    sk, sv = C_HEADS * C_DK, C_HEADS * C_DV
    w = w_in.astype(BF16)
    x2d = h3d.reshape(b * s, d)
    ws = [w[:, :sk], w[:, sk:2 * sk], w[:, 2 * sk:2 * sk + sv], w[:, 2 * sk + sv:]]
    q, k, v, g = norm_proj(x2d, g_norm, ws, [BF16, BF16, BF16, BF16])
    y = retention_core(q.reshape(b, s, sk), k.reshape(b, s, sk), v.reshape(b, s, sv), g.reshape(b, s, sv),
                       positions, g_out)
    return out_proj(y.reshape(b * s, sv), w_out.astype(BF16), x2d).reshape(b, s, d)


def _rglru_kernel(xb_ref, gb_ref, cw_ref, cb_ref, wg_ref, bg_ref, lru_ref, y_ref):
    s, bw = xb_ref.shape[1], xb_ref.shape[2]
    x = xb_ref[0]
    t = lax.broadcasted_iota(jnp.int32, (s, bw), 0)
    xc = x * cw_ref[D_CONV - 1:D_CONV, :] + cb_ref[...]
    for d in range(1, D_CONV):
        xc = xc + jnp.where(t >= d, pltpu.roll(x, d, 0), 0.0) * cw_ref[D_CONV - 1 - d:D_CONV - d, :]
    gates = _dot(xc.astype(BF16), wg_ref[0]) + bg_ref[0]
    r = _sigmoid(gates[:, :bw])
    i = _sigmoid(gates[:, bw:])
    log_a = -LRU_C * r * _softplus(-lru_ref[...])
    a = jnp.exp(log_a)
    bt = jnp.sqrt(jnp.tanh(-log_a) * (1.0 + a * a)) * (i * xc)
    d = 1
    while d < s:
        keep = t >= d
        a_sh = jnp.where(keep, pltpu.roll(a, d, 0), 1.0)
        b_sh = jnp.where(keep, pltpu.roll(bt, d, 0), 0.0)
        bt = a * b_sh + bt
        a = a * a_sh
        d *= 2
    y_ref[0] = (bt * jax.nn.gelu(gb_ref[0].astype(F32))).astype(y_ref.dtype)


def rglru_core(gate_br, xb, conv_w, conv_b, w_gates, b_gates, lru):
    b, s, dr = xb.shape
    bw = D_BW
    chan = lambda i, n: (i, 0, n)
    return pl.pallas_call(
        _rglru_kernel, grid=(b, dr // bw),
        in_specs=[pl.BlockSpec((1, s, bw), chan), pl.BlockSpec((1, s, bw), chan),
                  pl.BlockSpec((D_CONV, bw), lambda i, n: (0, n)), pl.BlockSpec((1, bw), lambda i, n: (0, n)),
                  pl.BlockSpec((1, bw, 2 * bw), lambda i, n: (n, 0, 0)),
                  pl.BlockSpec((1, 1, 2 * bw), lambda i, n: (n, 0, 0)),
                  pl.BlockSpec((1, bw), lambda i, n: (0, n))],
        out_specs=pl.BlockSpec((1, s, bw), chan),
        out_shape=jax.ShapeDtypeStruct((b, s, dr), BF16),
        compiler_params=_params("parallel", "parallel"), name="rglru_core",
    )(xb, gate_br, conv_w, conv_b.reshape(1, dr), w_gates.astype(BF16), b_gates.reshape(D_BLOCKS, 1, 2 * bw),
      lru.reshape(1, dr))


def mixer_d(h3d, g_norm, w_in, conv_w, conv_b, w_gates, b_gates, lru, w_out):
    b, s, d = h3d.shape
    dr = D_BLOCKS * D_BW
    w = w_in.astype(BF16)
    x2d = h3d.reshape(b * s, d)
    gate_br, xb = norm_proj(x2d, g_norm, [w[:, :dr], w[:, dr:]], [BF16, F32])
    y = rglru_core(gate_br.reshape(b, s, dr), xb.reshape(b, s, dr), conv_w, conv_b, w_gates, b_gates, lru)
    return out_proj(y.reshape(b * s, dr), w_out.astype(BF16), x2d).reshape(b, s, d)


def ffn_layer(h3d, g_norm, w_in, w_out):
    b, s, d = h3d.shape
    dff = w_out.shape[0]
    w = w_in.astype(BF16)
    return ffn(h3d.reshape(b * s, d), g_norm, w[:, :dff], w[:, dff:], w_out.astype(BF16)).reshape(b, s, d)


def kernel(x, positions, norm_mix, norm_ffn, ffn_w_in, ffn_w_out, a_w_in, a_b_gates, a_g_out, a_w_out, b_w_in, b_g_q, b_g_k, b_w_out, c_w_in, c_g_out, c_w_out, d_w_in, d_conv_w, d_conv_b, d_w_gates, d_b_gates, d_lru, d_w_out):
    h = x
    depth = norm_mix.shape[0]
    for layer in range(depth):
        mixer, r = layer % 4, layer // 4
        if mixer == 0:
            h = mixer_a(h, norm_mix[layer], a_w_in[r], a_b_gates[r], a_g_out[r], a_w_out[r])
        elif mixer == 1:
            h = mixer_b(h, positions, norm_mix[layer], b_w_in[r], b_g_q[r], b_g_k[r], b_w_out[r])
        elif mixer == 2:
            h = mixer_c(h, positions, norm_mix[layer], c_w_in[r], c_g_out[r], c_w_out[r])
        else:
            h = mixer_d(h, norm_mix[layer], d_w_in[r], d_conv_w[r], d_conv_b[r], d_w_gates[r], d_b_gates[r],
                        d_lru[r], d_w_out[r])
        h = ffn_layer(h, norm_ffn[layer], ffn_w_in[layer], ffn_w_out[layer])
    return h
```

```python
import functools
import math

import jax
import jax.numpy as jnp
from jax import lax
from jax.experimental import pallas as pl
from jax.experimental.pallas import tpu as pltpu

F32 = jnp.float32
BF16 = jnp.bfloat16
EPS = 1e-6
NEG_INF = float("-inf")
HIGHEST = lax.Precision.HIGHEST

LANES = 128
VMEM_LIMIT_BYTES = 56 * 1024 * 1024

A_HEADS, A_DQK, A_DV = 4, 128, 256
A_CHUNK = 256
B_HEADS, B_DH, B_BLOCK, B_TOPK, B_QBLOCK = 8, 128, 256, 3, 128
B_ROPE_THETA, B_ROPE_DIMS = 500000.0, 32
C_HEADS, C_DK, C_DV, C_CHUNK = 4, 256, 512, 256
C_ROPE_THETA = 10000.0
D_BLOCKS, D_BW, D_CONV, LRU_C = 4, 256, 4, 8.0


def _params(*semantics):
    return pltpu.CompilerParams(dimension_semantics=semantics, vmem_limit_bytes=VMEM_LIMIT_BYTES)


def _row_tile(t, want=512):
    return want if t % want == 0 else t


def _sigmoid(x):
    return 0.5 * jnp.tanh(0.5 * x) + 0.5


def _softplus(x):
    return jnp.maximum(x, 0.0) + jnp.log1p(jnp.exp(-jnp.abs(x)))


def _rms(x, g):
    return x * lax.rsqrt(jnp.mean(x * x, axis=-1, keepdims=True) + EPS) * g


def _dot(a, b):
    return jnp.dot(a, b, preferred_element_type=F32)


def _dot_nt(a, b):
    return lax.dot_general(a, b, (((1,), (1,)), ((), ())), preferred_element_type=F32)


def _norm_proj_kernel(x_ref, g_ref, *refs):
    n = len(refs) // 2
    u = _rms(x_ref[...], g_ref[...]).astype(BF16)
    for w_ref, o_ref in zip(refs[:n], refs[n:]):
        o_ref[...] = _dot(u, w_ref[...]).astype(o_ref.dtype).reshape(o_ref.shape)


def norm_proj(x2d, g, weights, out_dtypes, segments=None):
    t, d = x2d.shape
    tm = _row_tile(t) if segments is None else t // (segments[0] * segments[1])
    in_specs = [pl.BlockSpec((tm, d), lambda i: (i, 0)), pl.BlockSpec((1, d), lambda i: (0, 0))]
    in_specs += [pl.BlockSpec(w.shape, lambda i: (0, 0)) for w in weights]
    if segments is None:
        out_specs = [pl.BlockSpec((tm, w.shape[1]), lambda i: (i, 0)) for w in weights]
        out_shape = [jax.ShapeDtypeStruct((t, w.shape[1]), dt) for w, dt in zip(weights, out_dtypes)]
    else:
        b, nseg = segments
        out_specs = [pl.BlockSpec((1, tm, w.shape[1]), lambda i: (i // nseg, 0, i % nseg)) for w in weights]
        out_shape = [jax.ShapeDtypeStruct((b, tm, nseg * w.shape[1]), dt) for w, dt in zip(weights, out_dtypes)]
    return pl.pallas_call(
        _norm_proj_kernel, grid=(t // tm,), in_specs=in_specs, out_specs=out_specs, out_shape=out_shape,
        compiler_params=_params("parallel"), name="norm_proj",
    )(x2d, g.reshape(1, d), *weights)


def _out_proj_kernel(y_ref, w_ref, r_ref, o_ref):
    y = y_ref[...]
    o_ref[...] = r_ref[...] + _dot(y.reshape(y.shape[-2:]), w_ref[...])


def out_proj(y, w, res2d, segments=None):
    t, d = res2d.shape
    k = w.shape[0]
    if segments is None:
        tm = _row_tile(t)
        y_spec = pl.BlockSpec((tm, k), lambda i: (i, 0))
    else:
        b, nseg = segments
        tm = t // (b * nseg)
        y_spec = pl.BlockSpec((1, tm, k), lambda i: (i // nseg, 0, i % nseg))
    return pl.pallas_call(
        _out_proj_kernel, grid=(t // tm,),
        in_specs=[y_spec, pl.BlockSpec((k, d), lambda i: (0, 0)), pl.BlockSpec((tm, d), lambda i: (i, 0))],
        out_specs=pl.BlockSpec((tm, d), lambda i: (i, 0)),
        out_shape=jax.ShapeDtypeStruct((t, d), F32),
        compiler_params=_params("parallel"), name="out_proj",
    )(y, w, res2d)


def _ffn_kernel(x_ref, g_ref, wg_ref, wu_ref, wo_ref, o_ref, *, n_chunks):
    x = x_ref[...]
    u = _rms(x, g_ref[...]).astype(BF16)
    fc = wg_ref.shape[1] // n_chunks
    acc = x
    for c in range(n_chunks):
        gt = _dot(u, wg_ref[:, c * fc:(c + 1) * fc])
        up = _dot(u, wu_ref[:, c * fc:(c + 1) * fc])
        act = (gt * _sigmoid(gt) * up).astype(BF16)
        acc = acc + _dot(act, wo_ref[c * fc:(c + 1) * fc, :])
    o_ref[...] = acc


def ffn(x2d, g, w_gate, w_up, w_out):
    t, d = x2d.shape
    dff = w_gate.shape[1]
    tm = _row_tile(t)
    n_chunks = 2 if dff % (2 * LANES) == 0 else 1
    return pl.pallas_call(
        functools.partial(_ffn_kernel, n_chunks=n_chunks), grid=(t // tm,),
        in_specs=[pl.BlockSpec((tm, d), lambda i: (i, 0)), pl.BlockSpec((1, d), lambda i: (0, 0)),
                  pl.BlockSpec((d, dff), lambda i: (0, 0)), pl.BlockSpec((d, dff), lambda i: (0, 0)),
                  pl.BlockSpec((dff, d), lambda i: (0, 0))],
        out_specs=pl.BlockSpec((tm, d), lambda i: (i, 0)),
        out_shape=jax.ShapeDtypeStruct((t, d), F32),
        compiler_params=_params("parallel"), name="ffn",
    )(x2d, g.reshape(1, d), w_gate, w_up, w_out)


def _mlstm_kernel(q_ref, k_ref, v_ref, o_ref, gc_ref, gr_ref, bc_ref, br_ref, gout_ref, y_ref,
                  ct_s, n_s, m_s):
    l = q_ref.shape[1]
    h, dqk, dv = A_HEADS, A_DQK, A_DV
    scale = dqk ** -0.5

    @pl.when(pl.program_id(1) == 0)
    def _():
        ct_s[...] = jnp.zeros_like(ct_s)
        n_s[...] = jnp.zeros_like(n_s)
        m_s[...] = jnp.zeros_like(m_s)

    gc = gc_ref[0] + bc_ref[...]
    gr = gr_ref[0, 0] + br_ref[...]
    lf_c = jnp.minimum(gc, 0.0) - jnp.log1p(jnp.exp(-jnp.abs(gc)))
    lf_r = jnp.minimum(gr, 0.0) - jnp.log1p(jnp.exp(-jnp.abs(gr)))
    row = lax.broadcasted_iota(jnp.int32, (l, l), 0)
    col = lax.broadcasted_iota(jnp.int32, (l, l), 1)
    causal = row >= col
    bcum_c = jnp.dot(causal.astype(F32), lf_c, precision=HIGHEST, preferred_element_type=F32)
    bcum_r = jnp.dot(lf_r, (row <= col).astype(F32), precision=HIGHEST, preferred_element_type=F32)

    q = q_ref[0]
    k = k_ref[0]
    v = v_ref[0]
    og = o_ref[0].astype(F32)
    for hh in range(h):
        bc = bcum_c[:, h + hh:h + hh + 1]
        br = bcum_r[h + hh:h + hh + 1, :]
        ic = gc[:, hh:hh + 1]
        ir = gr[hh:hh + 1, :]
        btot = bc[l - 1:l, :]
        m_st = m_s[hh][:, :1]
        qh = q[:, hh * dqk:(hh + 1) * dqk]
        kh = k[:, hh * dqk:(hh + 1) * dqk]
        vh = v[:, hh * dv:(hh + 1) * dv]
        ct = ct_s[hh]
        nst = n_s[hh]

        dmat = jnp.where(causal, bc - br + ir, NEG_INF)
        inter = bc + m_st
        m_t = jnp.maximum(inter, jnp.max(dmat, axis=1, keepdims=True))
        w_intra = jnp.exp(dmat - m_t) * scale
        w_inter = jnp.exp(inter - m_t) * scale
        qk = _dot_nt(qh, kh) * w_intra
        num = _dot(qk.astype(BF16), vh) + w_inter * _dot(qh, ct.astype(BF16))
        den = jnp.sum(qk, axis=1, keepdims=True) + w_inter * jnp.sum(qh.astype(F32) * nst, axis=1, keepdims=True)
        hout = num / jnp.maximum(jnp.abs(den), jnp.exp(-m_t))

        dec = btot - bc + ic
        m_new = jnp.maximum(btot + m_st, jnp.max(dec, axis=0, keepdims=True))
        ws = jnp.exp(dec - m_new)
        wc = jnp.exp(btot + m_st - m_new)
        kw = kh.astype(F32) * ws
        ct_s[hh] = wc * ct + _dot(kw.T.astype(BF16), vh)
        n_s[hh] = wc * nst + jnp.sum(kw, axis=0, keepdims=True)
        m_s[hh] = jnp.broadcast_to(m_new, (1, LANES))

        hn = _rms(hout, gout_ref[hh:hh + 1, :])
        y_ref[0, :, hh * dv:(hh + 1) * dv] = (hn * _sigmoid(og[:, hh * dv:(hh + 1) * dv])).astype(y_ref.dtype)


def mlstm_core(q, k, v, o, gates, b_gates, g_out):
    b, s, _ = q.shape
    h, l = A_HEADS, A_CHUNK if s % A_CHUNK == 0 else s
    nc = s // l
    gates_row = gates[..., :2 * h].reshape(b, nc, l, 2 * h).transpose(0, 1, 3, 2)
    bc = jnp.zeros((1, LANES), F32).at[0, :2 * h].set(b_gates)
    br = b_gates.reshape(2 * h, 1)
    sq, sv = h * A_DQK, h * A_DV
    return pl.pallas_call(
        _mlstm_kernel, grid=(b, nc),
        in_specs=[pl.BlockSpec((1, l, sq), lambda i, c: (i, c, 0)), pl.BlockSpec((1, l, sq), lambda i, c: (i, c, 0)),
                  pl.BlockSpec((1, l, sv), lambda i, c: (i, c, 0)), pl.BlockSpec((1, l, sv), lambda i, c: (i, c, 0)),
                  pl.BlockSpec((1, l, LANES), lambda i, c: (i, c, 0)),
                  pl.BlockSpec((1, 1, 2 * h, l), lambda i, c: (i, c, 0, 0)),
                  pl.BlockSpec((1, LANES), lambda i, c: (0, 0)), pl.BlockSpec((2 * h, 1), lambda i, c: (0, 0)),
                  pl.BlockSpec((h, A_DV), lambda i, c: (0, 0))],
        out_specs=pl.BlockSpec((1, l, sv), lambda i, c: (i, c, 0)),
        out_shape=jax.ShapeDtypeStruct((b, s, sv), BF16),
        scratch_shapes=[pltpu.VMEM((h, A_DQK, A_DV), F32), pltpu.VMEM((h, 1, A_DQK), F32),
                        pltpu.VMEM((h, 1, LANES), F32)],
        compiler_params=_params("parallel", "arbitrary"), name="mlstm_core",
    )(q, k, v, o, gates, gates_row, bc, br, g_out)


def mixer_a(h3d, g_norm, w_in, b_gates, g_out, w_out):
    b, s, d = h3d.shape
    hh = A_HEADS
    sq, sv = hh * A_DQK, hh * A_DV
    w = w_in.astype(BF16)
    w_gates = jnp.zeros((d, LANES), BF16).at[:, :2 * hh].set(w[:, 2 * sq + 2 * sv:])
    ws = [w[:, :sq], w[:, sq:2 * sq], w[:, 2 * sq:2 * sq + sv], w[:, 2 * sq + sv:2 * sq + 2 * sv], w_gates]
    x2d = h3d.reshape(b * s, d)
    q, k, v, o, gates = norm_proj(x2d, g_norm, ws, [BF16, BF16, BF16, BF16, F32])
    y = mlstm_core(q.reshape(b, s, sq), k.reshape(b, s, sq), v.reshape(b, s, sv), o.reshape(b, s, sv),
                   gates.reshape(b, s, LANES), b_gates, g_out)
    return out_proj(y.reshape(b * s, sv), w_out.astype(BF16), x2d).reshape(b, s, d)


def _moba_kernel(q_ref, k_ref, v_ref, pos_ref, inv_ref, gq_ref, gk_ref, o_ref, cos_s, sin_s, km_s, *, topk):
    s, dh = q_ref.shape[1], q_ref.shape[2]
    nblk = s // B_BLOCK
    half = B_ROPE_DIMS // 2
    first = lax.broadcasted_iota(jnp.int32, (s, dh), 1) < half

    @pl.when(pl.program_id(1) == 0)
    def _():
        ang = pos_ref[0] * inv_ref[...]
        cos_s[...] = jnp.cos(ang)
        sin_s[...] = jnp.where(first, -jnp.sin(ang), jnp.sin(ang))

    def rot(x):
        partner = jnp.where(first, pltpu.roll(x, dh - half, 1), pltpu.roll(x, half, 1))
        return x * cos_s[...] + partner * sin_s[...]

    qn = rot(_rms(q_ref[0], gq_ref[...])) * (dh ** -0.5)
    kn = rot(_rms(k_ref[0], gk_ref[...]))
    qb16 = qn.astype(BF16)
    kb16 = kn.astype(BF16)
    km_s[...] = jnp.zeros_like(km_s)
    for n in range(nblk):
        km_s[n:n + 1, :] = jnp.mean(kn[n * B_BLOCK:(n + 1) * B_BLOCK, :], axis=0, keepdims=True)

    blk = lax.broadcasted_iota(jnp.int32, (B_BLOCK, LANES), 1)
    tri = (lax.broadcasted_iota(jnp.int32, (B_BLOCK, B_BLOCK), 0)
           >= lax.broadcasted_iota(jnp.int32, (B_BLOCK, B_BLOCK), 1))
    for j in range(nblk):
        rows = slice(j * B_BLOCK, (j + 1) * B_BLOCK)
        qj = qb16[rows]
        select = j > topk
        if select:
            gate = lax.dot_general(qn[rows], km_s[...], (((1,), (1,)), ((), ())), precision=HIGHEST,
                                   preferred_element_type=F32)
            gate = jnp.where(blk < j, gate, NEG_INF)
            rank = jnp.zeros(gate.shape, F32)
            for m in range(j):
                gm = gate[:, m:m + 1]
                rank = rank + jnp.where((gm > gate) | ((gm == gate) & (blk > m)), 1.0, 0.0)
            bias = jnp.where(rank < topk, 0.0, NEG_INF)
        parts = []
        for n in range(j):
            sc = _dot_nt(qj, kb16[n * B_BLOCK:(n + 1) * B_BLOCK])
            parts.append(sc + bias[:, n:n + 1] if select else sc)
        parts.append(jnp.where(tri, _dot_nt(qj, kb16[rows]), NEG_INF))
        sc = jnp.concatenate(parts, axis=1) if j else parts[0]
        p = jnp.exp(sc - jnp.max(sc, axis=1, keepdims=True))
        out = _dot(p.astype(BF16), v_ref[0, :(j + 1) * B_BLOCK, :]) / jnp.sum(p, axis=1, keepdims=True)
        o_ref[0, rows, :] = out.astype(o_ref.dtype)


def moba_core(q, k, v, positions, g_q, g_k):
    b, s, hd = q.shape
    h, dh = B_HEADS, B_DH
    assert s % B_BLOCK == 0 and s // B_BLOCK <= LANES
    topk = min(B_TOPK, s // B_BLOCK - 1)
    half = B_ROPE_DIMS // 2
    inv = B_ROPE_THETA ** (-jnp.arange(half, dtype=F32) * (2.0 / B_ROPE_DIMS))
    inv_row = jnp.zeros((1, dh), F32).at[0, :half].set(inv).at[0, half:2 * half].set(inv)
    pos = positions.astype(F32).reshape(b, s, 1)
    head_spec = pl.BlockSpec((1, s, dh), lambda i, hh: (i, 0, hh))
    const = lambda i, hh: (0, 0)
    return pl.pallas_call(
        functools.partial(_moba_kernel, topk=topk), grid=(b, h),
        in_specs=[head_spec, head_spec, head_spec, pl.BlockSpec((1, s, 1), lambda i, hh: (i, 0, 0)),
                  pl.BlockSpec((1, dh), const), pl.BlockSpec((1, dh), const), pl.BlockSpec((1, dh), const)],
        out_specs=head_spec,
        out_shape=jax.ShapeDtypeStruct((b, s, hd), BF16),
        scratch_shapes=[pltpu.VMEM((s, dh), F32), pltpu.VMEM((s, dh), F32), pltpu.VMEM((LANES, dh), F32)],
        compiler_params=_params("parallel", "arbitrary"), name="moba_core",
    )(q, k, v, pos, inv_row, g_q.reshape(1, dh), g_k.reshape(1, dh))


def mixer_b(h3d, positions, g_norm, w_in, g_q, g_k, w_out):
    b, s, d = h3d.shape
    hd = B_HEADS * B_DH
    w = w_in.astype(BF16)
    x2d = h3d.reshape(b * s, d)
    q, k, v = norm_proj(x2d, g_norm, [w[:, :hd], w[:, hd:2 * hd], w[:, 2 * hd:]], [F32, F32, BF16])
    y = moba_core(q.reshape(b, s, hd), k.reshape(b, s, hd), v.reshape(b, s, hd), positions, g_q, g_k)
    return out_proj(y.reshape(b * s, hd), w_out.astype(BF16), x2d).reshape(b, s, d)


def _retention_kernel(q_ref, k_ref, v_ref, g_ref, pos_ref, inv_ref, gout_ref, y_ref, r_s):
    l = q_ref.shape[1]
    h, dk, dv = C_HEADS, C_DK, C_DV
    half = dk // 2

    @pl.when(pl.program_id(1) == 0)
    def _():
        r_s[...] = jnp.zeros_like(r_s)

    ang = pos_ref[0] * inv_ref[...]
    cos, sin = jnp.cos(ang), jnp.sin(ang)
    row = lax.broadcasted_iota(jnp.int32, (l, l), 0)
    col = lax.broadcasted_iota(jnp.int32, (l, l), 1)
    diff = (row - col).astype(F32)
    t_col = lax.broadcasted_iota(jnp.int32, (l, 1), 0).astype(F32)

    def rot(x):
        x1, x2 = x[:, :half], x[:, half:]
        return jnp.concatenate([x1 * cos - x2 * sin, x2 * cos + x1 * sin], axis=1)

    q = q_ref[0].astype(F32)
    k = k_ref[0].astype(F32)
    v = v_ref[0]
    g = g_ref[0].astype(F32)
    for hh in range(h):
        log_g = math.log1p(-(2.0 ** (-5.0 - hh)))
        intra = jnp.where(diff >= 0, jnp.exp(jnp.maximum(diff, 0.0) * log_g), 0.0)
        xi = jnp.exp((t_col + 1.0) * log_g)
        zeta = jnp.exp((l - 1.0 - t_col) * log_g)
        qh = rot(q[:, hh * dk:(hh + 1) * dk]).astype(BF16)
        kf = rot(k[:, hh * dk:(hh + 1) * dk]) * (dk ** -0.5)
        vh = v[:, hh * dv:(hh + 1) * dv]
        r_st = r_s[hh]
        inner = _dot((_dot_nt(qh, kf.astype(BF16)) * intra).astype(BF16), vh)
        cross = _dot(qh, r_st.astype(BF16)) * xi
        r_s[hh] = math.exp(l * log_g) * r_st + _dot((kf * zeta).T.astype(BF16), vh)
        ys = inner + cross
        mu = jnp.mean(ys, axis=1, keepdims=True)
        yc = ys - mu
        yn = yc * lax.rsqrt(jnp.mean(yc * yc, axis=1, keepdims=True) + EPS) * gout_ref[hh:hh + 1, :]
        gh = g[:, hh * dv:(hh + 1) * dv]
        y_ref[0, :, hh * dv:(hh + 1) * dv] = (yn * (gh * _sigmoid(gh))).astype(y_ref.dtype)


def retention_core(q, k, v, g, positions, g_out):
    b, s, _ = q.shape
    h, l = C_HEADS, C_CHUNK if s % C_CHUNK == 0 else s
    sk, sv = h * C_DK, h * C_DV
    half = C_DK // 2
    inv = (C_ROPE_THETA ** (-jnp.arange(half, dtype=F32) * (2.0 / C_DK))).reshape(1, half)
    pos = positions.astype(F32).reshape(b, s, 1)
    return pl.pallas_call(
        _retention_kernel, grid=(b, s // l),
        in_specs=[pl.BlockSpec((1, l, sk), lambda i, c: (i, c, 0)), pl.BlockSpec((1, l, sk), lambda i, c: (i, c, 0)),
                  pl.BlockSpec((1, l, sv), lambda i, c: (i, c, 0)), pl.BlockSpec((1, l, sv), lambda i, c: (i, c, 0)),
                  pl.BlockSpec((1, l, 1), lambda i, c: (i, c, 0)), pl.BlockSpec((1, half), lambda i, c: (0, 0)),
                  pl.BlockSpec((h, C_DV), lambda i, c: (0, 0))],
        out_specs=pl.BlockSpec((1, l, sv), lambda i, c: (i, c, 0)),
        out_shape=jax.ShapeDtypeStruct((b, s, sv), BF16),
        scratch_shapes=[pltpu.VMEM((h, C_DK, C_DV), F32)],
        compiler_params=_params("parallel", "arbitrary"), name="retention_core",
    )(q, k, v, g, pos, inv, g_out)


def mixer_c(h3d, positions, g_norm, w_in, g_out, w_out):
    b, s, d = h3d.shape
    sk, sv = C_HEADS * C_DK, C_HEADS * C_DV
    w = w_in.astype(BF16)
    x2d = h3d.reshape(b * s, d)
    ws = [w[:, :sk], w[:, sk:2 * sk], w[:, 2 * sk:2 * sk + sv], w[:, 2 * sk + sv:]]
    q, k, v, g = norm_proj(x2d, g_norm, ws, [BF16, BF16, BF16, BF16])
    y = retention_core(q.reshape(b, s, sk), k.reshape(b, s, sk), v.reshape(b, s, sv), g.reshape(b, s, sv),
                       positions, g_out)
    return out_proj(y.reshape(b * s, sv), w_out.astype(BF16), x2d).reshape(b, s, d)


D_SEGMENTS = 8


def _rglru_kernel(xb_ref, gb_ref, cw_ref, cb_ref, wg_ref, bg_ref, lru_ref, y_ref, a_s, b_s):
    rows, c = xb_ref.shape[1], xb_ref.shape[2]
    nseg, bw = D_SEGMENTS, D_BW
    seg = rows // nseg
    x = xb_ref[0]
    ntail = (D_CONV - 1) * nseg
    tail = x[rows - ntail:, :]
    seg_id = lax.broadcasted_iota(jnp.int32, (ntail, c), 0) & (nseg - 1)
    head = jnp.where(seg_id >= 1, pltpu.roll(tail, 1, 0), 0.0)
    xe = jnp.concatenate([head, x], axis=0)
    xc = x * cw_ref[D_CONV - 1:D_CONV, :] + cb_ref[...]
    for d in range(1, D_CONV):
        xc = xc + xe[ntail - d * nseg:ntail - d * nseg + rows, :] * cw_ref[D_CONV - 1 - d:D_CONV - d, :]
    sp = _softplus(-lru_ref[...])
    for n in range(c // bw):
        cols = slice(n * bw, (n + 1) * bw)
        xcn = xc[:, cols]
        gates = _dot(xcn.astype(BF16), wg_ref[n]) + bg_ref[n]
        r = _sigmoid(gates[:, :bw])
        i = _sigmoid(gates[:, bw:])
        log_a = (-LRU_C * sp[:, cols]) * r
        a = jnp.exp(log_a)
        a_s[:, cols] = a
        var = jnp.tanh(-log_a) * (1.0 + a * a)
        b_s[:, cols] = jnp.where(var > 0.0, var * lax.rsqrt(var), 0.0) * (i * xcn)

    def slab(m):
        return pl.ds(pl.multiple_of(m * nseg, nseg), nseg)

    def totals(m, carry):
        h, acc = carry
        am = a_s[slab(m), :]
        return am * h + b_s[slab(m), :], am * acc

    h_loc, a_tot = lax.fori_loop(0, seg, totals, (jnp.zeros((nseg, c), F32), jnp.ones((nseg, c), F32)), unroll=8)
    carry_in = [jnp.zeros((1, c), F32)]
    for r in range(1, nseg):
        carry_in.append(h_loc[r - 1:r, :] + a_tot[r - 1:r, :] * carry_in[-1])

    def scan(m, h):
        h = a_s[slab(m), :] * h + b_s[slab(m), :]
        b_s[slab(m), :] = h
        return h

    lax.fori_loop(0, seg, scan, jnp.concatenate(carry_in, axis=0), unroll=8)
    y_ref[0] = (b_s[...] * jax.nn.gelu(gb_ref[0].astype(F32))).astype(y_ref.dtype)


def rglru_core(gate_br, xb, conv_w, conv_b, w_gates, b_gates, lru):
    b, s, dr = xb.shape
    c = 2 * D_BW
    nb = c // D_BW
    chan = lambda i, n: (i, 0, n)
    return pl.pallas_call(
        _rglru_kernel, grid=(b, dr // c),
        in_specs=[pl.BlockSpec((1, s, c), chan), pl.BlockSpec((1, s, c), chan),
                  pl.BlockSpec((D_CONV, c), lambda i, n: (0, n)), pl.BlockSpec((1, c), lambda i, n: (0, n)),
                  pl.BlockSpec((nb, D_BW, 2 * D_BW), lambda i, n: (n, 0, 0)),
                  pl.BlockSpec((nb, 1, 2 * D_BW), lambda i, n: (n, 0, 0)),
                  pl.BlockSpec((1, c), lambda i, n: (0, n))],
        out_specs=pl.BlockSpec((1, s, c), chan),
        out_shape=jax.ShapeDtypeStruct((b, s, dr), BF16),
        scratch_shapes=[pltpu.VMEM((s, c), F32), pltpu.VMEM((s, c), F32)],
        compiler_params=_params("parallel", "parallel"), name="rglru_core",
    )(xb, gate_br, conv_w, conv_b.reshape(1, dr), w_gates.astype(BF16), b_gates.reshape(D_BLOCKS, 1, 2 * D_BW),
      lru.reshape(1, dr))


def mixer_d(h3d, g_norm, w_in, conv_w, conv_b, w_gates, b_gates, lru, w_out):
    b, s, d = h3d.shape
    dr = D_BLOCKS * D_BW
    assert s % (8 * D_SEGMENTS) == 0
    w = w_in.astype(BF16)
    x2d = h3d.reshape(b * s, d)
    segments = (b, D_SEGMENTS)
    gate_br, xb = norm_proj(x2d, g_norm, [w[:, :dr], w[:, dr:]], [BF16, F32], segments=segments)
    y = rglru_core(gate_br.reshape(b, s, dr), xb.reshape(b, s, dr), conv_w, conv_b, w_gates, b_gates, lru)
    y = y.reshape(b, s // D_SEGMENTS, D_SEGMENTS * dr)
    return out_proj(y, w_out.astype(BF16), x2d, segments=segments).reshape(b, s, d)


def ffn_layer(h3d, g_norm, w_in, w_out):
    b, s, d = h3d.shape
    dff = w_out.shape[0]
    w = w_in.astype(BF16)
    return ffn(h3d.reshape(b * s, d), g_norm, w[:, :dff], w[:, dff:], w_out.astype(BF16)).reshape(b, s, d)


def kernel(x, positions, norm_mix, norm_ffn, ffn_w_in, ffn_w_out, a_w_in, a_b_gates, a_g_out, a_w_out, b_w_in, b_g_q, b_g_k, b_w_out, c_w_in, c_g_out, c_w_out, d_w_in, d_conv_w, d_conv_b, d_w_gates, d_b_gates, d_lru, d_w_out):
    h = x
    depth = norm_mix.shape[0]
    for layer in range(depth):
        mixer, r = layer % 4, layer // 4
        if mixer == 0:
            h = mixer_a(h, norm_mix[layer], a_w_in[r], a_b_gates[r], a_g_out[r], a_w_out[r])
        elif mixer == 1:
            h = mixer_b(h, positions, norm_mix[layer], b_w_in[r], b_g_q[r], b_g_k[r], b_w_out[r])
        elif mixer == 2:
            h = mixer_c(h, positions, norm_mix[layer], c_w_in[r], c_g_out[r], c_w_out[r])
        else:
            h = mixer_d(h, norm_mix[layer], d_w_in[r], d_conv_w[r], d_conv_b[r], d_w_gates[r], d_b_gates[r],
                        d_lru[r], d_w_out[r])
        h = ffn_layer(h, norm_ffn[layer], ffn_w_in[layer], ffn_w_out[layer])
    return h
```

```python
import functools
import math

import jax
import jax.numpy as jnp
from jax import lax
from jax.experimental import pallas as pl
from jax.experimental.pallas import tpu as pltpu

F32 = jnp.float32
BF16 = jnp.bfloat16
EPS = 1e-6
NEG_INF = float("-inf")
HIGHEST = lax.Precision.HIGHEST

LANES = 128
VMEM_LIMIT_BYTES = 56 * 1024 * 1024

A_HEADS, A_DQK, A_DV = 4, 128, 256
A_CHUNK = 256
B_HEADS, B_DH, B_BLOCK, B_TOPK, B_QBLOCK = 8, 128, 256, 3, 128
B_ROPE_THETA, B_ROPE_DIMS = 500000.0, 32
C_HEADS, C_DK, C_DV, C_CHUNK = 4, 256, 512, 256
C_ROPE_THETA = 10000.0
D_BLOCKS, D_BW, D_CONV, LRU_C = 4, 256, 4, 8.0


def _params(*semantics):
    return pltpu.CompilerParams(dimension_semantics=semantics, vmem_limit_bytes=VMEM_LIMIT_BYTES)


def _row_tile(t, want=512):
    return want if t % want == 0 else t


def _sigmoid(x):
    return 0.5 * jnp.tanh(0.5 * x) + 0.5


def _silu(x):
    return x * _sigmoid(x)


def _softplus(x):
    return jnp.maximum(x, 0.0) + jnp.log1p(jnp.exp(-jnp.abs(x)))


def _rms(x, g):
    return x * lax.rsqrt(jnp.mean(x * x, axis=-1, keepdims=True) + EPS) * g


def _dot(a, b):
    return jnp.dot(a, b, preferred_element_type=F32)


def _dot_nt(a, b):
    return lax.dot_general(a, b, (((1,), (1,)), ((), ())), preferred_element_type=F32)


def _norm_proj_kernel(x_ref, g_ref, *refs, n, epilogue):
    w_refs, o_refs, aux_refs = refs[:n], refs[len(refs) - n:], refs[n:len(refs) - n]
    u = _rms(x_ref[...], g_ref[...]).astype(BF16)
    ys = [_dot(u, w_ref[...]) for w_ref in w_refs]
    if epilogue is not None:
        ys = epilogue(ys, *aux_refs)
    for y, o_ref in zip(ys, o_refs):
        o_ref[...] = y.astype(o_ref.dtype)


def norm_proj(x2d, g, weights, out_dtypes, epilogue=None, row_aux=(), const_aux=()):
    t, d = x2d.shape
    tm = _row_tile(t)
    in_specs = [pl.BlockSpec((tm, d), lambda i: (i, 0)), pl.BlockSpec((1, d), lambda i: (0, 0))]
    in_specs += [pl.BlockSpec(w.shape, lambda i: (0, 0)) for w in weights]
    in_specs += [pl.BlockSpec((tm, a.shape[1]), lambda i: (i, 0)) for a in row_aux]
    in_specs += [pl.BlockSpec(a.shape, lambda i: (0, 0)) for a in const_aux]
    out_specs = [pl.BlockSpec((tm, w.shape[1]), lambda i: (i, 0)) for w in weights]
    out_shape = [jax.ShapeDtypeStruct((t, w.shape[1]), dt) for w, dt in zip(weights, out_dtypes)]
    return pl.pallas_call(
        functools.partial(_norm_proj_kernel, n=len(weights), epilogue=epilogue), grid=(t // tm,),
        in_specs=in_specs, out_specs=out_specs, out_shape=out_shape,
        compiler_params=_params("parallel"), name="norm_proj",
    )(x2d, g.reshape(1, d), *weights, *row_aux, *const_aux)


def _out_proj_kernel(y_ref, w_ref, r_ref, o_ref):
    o_ref[...] = r_ref[...] + _dot(y_ref[...], w_ref[...])


def out_proj(y, w, res2d):
    t, d = res2d.shape
    k = w.shape[0]
    tm = _row_tile(t)
    return pl.pallas_call(
        _out_proj_kernel, grid=(t // tm,),
        in_specs=[pl.BlockSpec((tm, k), lambda i: (i, 0)), pl.BlockSpec((k, d), lambda i: (0, 0)),
                  pl.BlockSpec((tm, d), lambda i: (i, 0))],
        out_specs=pl.BlockSpec((tm, d), lambda i: (i, 0)),
        out_shape=jax.ShapeDtypeStruct((t, d), F32),
        compiler_params=_params("parallel"), name="out_proj",
    )(y, w, res2d)


def _ffn_kernel(x_ref, g_ref, wg_ref, wu_ref, wo_ref, o_ref, *, n_chunks):
    x = x_ref[...]
    u = _rms(x, g_ref[...]).astype(BF16)
    fc = wg_ref.shape[1] // n_chunks
    acc = x
    for c in range(n_chunks):
        gt = _dot(u, wg_ref[:, c * fc:(c + 1) * fc])
        up = _dot(u, wu_ref[:, c * fc:(c + 1) * fc])
        act = (_silu(gt) * up).astype(BF16)
        acc = acc + _dot(act, wo_ref[c * fc:(c + 1) * fc, :])
    o_ref[...] = acc


def ffn(x2d, g, w_gate, w_up, w_out):
    t, d = x2d.shape
    dff = w_gate.shape[1]
    tm = _row_tile(t)
    n_chunks = 2 if dff % (2 * LANES) == 0 else 1
    return pl.pallas_call(
        functools.partial(_ffn_kernel, n_chunks=n_chunks), grid=(t // tm,),
        in_specs=[pl.BlockSpec((tm, d), lambda i: (i, 0)), pl.BlockSpec((1, d), lambda i: (0, 0)),
                  pl.BlockSpec((d, dff), lambda i: (0, 0)), pl.BlockSpec((d, dff), lambda i: (0, 0)),
                  pl.BlockSpec((dff, d), lambda i: (0, 0))],
        out_specs=pl.BlockSpec((tm, d), lambda i: (i, 0)),
        out_shape=jax.ShapeDtypeStruct((t, d), F32),
        compiler_params=_params("parallel"), name="ffn",
    )(x2d, g.reshape(1, d), w_gate, w_up, w_out)


def _mlstm_kernel(q_ref, k_ref, v_ref, o_ref, gc_ref, gr_ref, bc_ref, br_ref, gout_ref, y_ref,
                  ct_s, n_s, m_s):
    l = q_ref.shape[1]
    h, dqk, dv = A_HEADS, A_DQK, A_DV
    scale = dqk ** -0.5

    @pl.when(pl.program_id(1) == 0)
    def _():
        ct_s[...] = jnp.zeros_like(ct_s)
        n_s[...] = jnp.zeros_like(n_s)
        m_s[...] = jnp.zeros_like(m_s)

    gc = gc_ref[0] + bc_ref[...]
    gr = gr_ref[0, 0] + br_ref[...]
    lf_c = jnp.minimum(gc, 0.0) - jnp.log1p(jnp.exp(-jnp.abs(gc)))
    lf_r = jnp.minimum(gr, 0.0) - jnp.log1p(jnp.exp(-jnp.abs(gr)))
    row = lax.broadcasted_iota(jnp.int32, (l, l), 0)
    col = lax.broadcasted_iota(jnp.int32, (l, l), 1)
    causal = row >= col
    bcum_c = jnp.dot(causal.astype(F32), lf_c, precision=HIGHEST, preferred_element_type=F32)
    bcum_r = jnp.dot(lf_r, (row <= col).astype(F32), precision=HIGHEST, preferred_element_type=F32)

    q = q_ref[0]
    k = k_ref[0]
    v = v_ref[0]
    og = o_ref[0].astype(F32)
    for hh in range(h):
        bc = bcum_c[:, h + hh:h + hh + 1]
        br = bcum_r[h + hh:h + hh + 1, :]
        ic = gc[:, hh:hh + 1]
        ir = gr[hh:hh + 1, :]
        btot = bc[l - 1:l, :]
        m_st = m_s[hh][:, :1]
        qh = q[:, hh * dqk:(hh + 1) * dqk]
        kh = k[:, hh * dqk:(hh + 1) * dqk]
        vh = v[:, hh * dv:(hh + 1) * dv]
        ct = ct_s[hh]
        nst = n_s[hh]

        dmat = jnp.where(causal, bc - br + ir, NEG_INF)
        inter = bc + m_st
        m_t = jnp.maximum(inter, jnp.max(dmat, axis=1, keepdims=True))
        w_intra = jnp.exp(dmat - m_t) * scale
        w_inter = jnp.exp(inter - m_t) * scale
        qk = _dot_nt(qh, kh) * w_intra
        num = _dot(qk.astype(BF16), vh) + w_inter * _dot(qh, ct.astype(BF16))
        den = jnp.sum(qk, axis=1, keepdims=True) + w_inter * jnp.sum(qh.astype(F32) * nst, axis=1, keepdims=True)
        hout = num / jnp.maximum(jnp.abs(den), jnp.exp(-m_t))

        dec = btot - bc + ic
        m_new = jnp.maximum(btot + m_st, jnp.max(dec, axis=0, keepdims=True))
        ws = jnp.exp(dec - m_new)
        wc = jnp.exp(btot + m_st - m_new)
        kw = kh.astype(F32) * ws
        ct_s[hh] = wc * ct + _dot(kw.T.astype(BF16), vh)
        n_s[hh] = wc * nst + jnp.sum(kw, axis=0, keepdims=True)
        m_s[hh] = jnp.broadcast_to(m_new, (1, LANES))

        hn = _rms(hout, gout_ref[hh:hh + 1, :])
        y_ref[0, :, hh * dv:(hh + 1) * dv] = (hn * og[:, hh * dv:(hh + 1) * dv]).astype(y_ref.dtype)


def mlstm_core(q, k, v, o, gates, b_gates, g_out):
    b, s, _ = q.shape
    h, l = A_HEADS, A_CHUNK if s % A_CHUNK == 0 else s
    nc = s // l
    gates_row = gates[..., :2 * h].reshape(b, nc, l, 2 * h).transpose(0, 1, 3, 2)
    bc = jnp.zeros((1, LANES), F32).at[0, :2 * h].set(b_gates)
    br = b_gates.reshape(2 * h, 1)
    sq, sv = h * A_DQK, h * A_DV
    return pl.pallas_call(
        _mlstm_kernel, grid=(b, nc),
        in_specs=[pl.BlockSpec((1, l, sq), lambda i, c: (i, c, 0)), pl.BlockSpec((1, l, sq), lambda i, c: (i, c, 0)),
                  pl.BlockSpec((1, l, sv), lambda i, c: (i, c, 0)), pl.BlockSpec((1, l, sv), lambda i, c: (i, c, 0)),
                  pl.BlockSpec((1, l, LANES), lambda i, c: (i, c, 0)),
                  pl.BlockSpec((1, 1, 2 * h, l), lambda i, c: (i, c, 0, 0)),
                  pl.BlockSpec((1, LANES), lambda i, c: (0, 0)), pl.BlockSpec((2 * h, 1), lambda i, c: (0, 0)),
                  pl.BlockSpec((h, A_DV), lambda i, c: (0, 0))],
        out_specs=pl.BlockSpec((1, l, sv), lambda i, c: (i, c, 0)),
        out_shape=jax.ShapeDtypeStruct((b, s, sv), BF16),
        scratch_shapes=[pltpu.VMEM((h, A_DQK, A_DV), F32), pltpu.VMEM((h, 1, A_DQK), F32),
                        pltpu.VMEM((h, 1, LANES), F32)],
        compiler_params=_params("parallel", "arbitrary"), name="mlstm_core",
    )(q, k, v, o, gates, gates_row, bc, br, g_out)


def _mlstm_epilogue(ys):
    o, q, k, v, gates = ys
    return [_sigmoid(o), q, k, v, gates]


def mixer_a(h3d, g_norm, w_in, b_gates, g_out, w_out):
    b, s, d = h3d.shape
    hh = A_HEADS
    sq, sv = hh * A_DQK, hh * A_DV
    w = w_in.astype(BF16)
    w_gates = jnp.zeros((d, LANES), BF16).at[:, :2 * hh].set(w[:, 2 * sq + 2 * sv:])
    ws = [w[:, 2 * sq + sv:2 * sq + 2 * sv], w[:, :sq], w[:, sq:2 * sq], w[:, 2 * sq:2 * sq + sv], w_gates]
    x2d = h3d.reshape(b * s, d)
    o, q, k, v, gates = norm_proj(x2d, g_norm, ws, [BF16, BF16, BF16, BF16, F32], epilogue=_mlstm_epilogue)
    y = mlstm_core(q.reshape(b, s, sq), k.reshape(b, s, sq), v.reshape(b, s, sv), o.reshape(b, s, sv),
                   gates.reshape(b, s, LANES), b_gates, g_out)
    return out_proj(y.reshape(b * s, sv), w_out.astype(BF16), x2d).reshape(b, s, d)


def _moba_kernel(q_ref, k_ref, v_ref, o_ref, km_s, *, topk):
    s, dh = q_ref.shape[1], q_ref.shape[2]
    nblk = s // B_BLOCK
    qn = q_ref[0]
    kn = k_ref[0]
    qb16 = qn.astype(BF16)
    kb16 = kn.astype(BF16)
    km_s[...] = jnp.zeros_like(km_s)
    for n in range(nblk):
        km_s[n:n + 1, :] = jnp.mean(kn[n * B_BLOCK:(n + 1) * B_BLOCK, :], axis=0, keepdims=True)

    nb8 = -(-nblk // 8) * 8
    blk = lax.broadcasted_iota(jnp.int32, (nb8, B_BLOCK), 0)
    tri = (lax.broadcasted_iota(jnp.int32, (B_BLOCK, B_BLOCK), 0)
           >= lax.broadcasted_iota(jnp.int32, (B_BLOCK, B_BLOCK), 1))
    for j in range(nblk):
        rows = slice(j * B_BLOCK, (j + 1) * B_BLOCK)
        qj = qb16[rows]
        select = j > topk
        if select:
            gate = lax.dot_general(km_s[:nb8, :], qn[rows], (((1,), (1,)), ((), ())), precision=HIGHEST,
                                   preferred_element_type=F32)
            gate = jnp.where(blk < j, gate, NEG_INF)
            rank = jnp.zeros(gate.shape, F32)
            for m in range(j):
                gm = gate[m:m + 1, :]
                rank = rank + jnp.where((gm > gate) | ((gm == gate) & (blk > m)), 1.0, 0.0)
            bias = jnp.where(rank < topk, 0.0, NEG_INF)
            bias = jnp.concatenate([bias, jnp.zeros((LANES - nb8, B_BLOCK), F32)], axis=0).T
        parts = []
        for n in range(j):
            sc = _dot_nt(qj, kb16[n * B_BLOCK:(n + 1) * B_BLOCK])
            parts.append(sc + bias[:, n:n + 1] if select else sc)
        parts.append(jnp.where(tri, _dot_nt(qj, kb16[rows]), NEG_INF))
        sc = jnp.concatenate(parts, axis=1) if j else parts[0]
        p = jnp.exp(sc - jnp.max(sc, axis=1, keepdims=True))
        out = _dot(p.astype(BF16), v_ref[0, :(j + 1) * B_BLOCK, :]) / jnp.sum(p, axis=1, keepdims=True)
        o_ref[0, rows, :] = out.astype(o_ref.dtype)


def moba_core(q, k, v):
    b, s, hd = q.shape
    h, dh = B_HEADS, B_DH
    assert s % B_BLOCK == 0 and s // B_BLOCK <= LANES
    topk = min(B_TOPK, s // B_BLOCK - 1)
    head_spec = pl.BlockSpec((1, s, dh), lambda i, hh: (i, 0, hh))
    return pl.pallas_call(
        functools.partial(_moba_kernel, topk=topk), grid=(b, h),
        in_specs=[head_spec, head_spec, head_spec],
        out_specs=head_spec,
        out_shape=jax.ShapeDtypeStruct((b, s, hd), BF16),
        scratch_shapes=[pltpu.VMEM((LANES, dh), F32)],
        compiler_params=_params("parallel", "parallel"), name="moba_core",
    )(q, k, v)


def _moba_epilogue(ys, pos_ref, inv_ref, gq_ref, gk_ref):
    q, k, v = ys
    dh, half = B_DH, B_ROPE_DIMS // 2
    ang = pos_ref[...] * inv_ref[...]
    first = lax.broadcasted_iota(jnp.int32, ang.shape, 1) < half
    cos = jnp.cos(ang)
    sin = jnp.where(first, -jnp.sin(ang), jnp.sin(ang))

    def prep(x, g_ref, scale):
        parts = []
        for hh in range(B_HEADS):
            xh = _rms(x[:, hh * dh:(hh + 1) * dh], g_ref[...])
            partner = jnp.where(first, pltpu.roll(xh, dh - half, 1), pltpu.roll(xh, half, 1))
            parts.append((xh * cos + partner * sin) * scale)
        return jnp.concatenate(parts, axis=1)

    return [prep(q, gq_ref, dh ** -0.5), prep(k, gk_ref, 1.0), v]


def mixer_b(h3d, positions, g_norm, w_in, g_q, g_k, w_out):
    b, s, d = h3d.shape
    hd, dh, half = B_HEADS * B_DH, B_DH, B_ROPE_DIMS // 2
    w = w_in.astype(BF16)
    x2d = h3d.reshape(b * s, d)
    inv = B_ROPE_THETA ** (-jnp.arange(half, dtype=F32) * (2.0 / B_ROPE_DIMS))
    inv_row = jnp.zeros((1, dh), F32).at[0, :half].set(inv).at[0, half:2 * half].set(inv)
    pos = positions.astype(F32).reshape(b * s, 1)
    q, k, v = norm_proj(x2d, g_norm, [w[:, :hd], w[:, hd:2 * hd], w[:, 2 * hd:]], [F32, F32, BF16],
                        epilogue=_moba_epilogue, row_aux=[pos], const_aux=[inv_row, g_q.reshape(1, dh), g_k.reshape(1, dh)])
    y = moba_core(q.reshape(b, s, hd), k.reshape(b, s, hd), v.reshape(b, s, hd))
    return out_proj(y.reshape(b * s, hd), w_out.astype(BF16), x2d).reshape(b, s, d)


def _retention_kernel(q_ref, k_ref, v_ref, g_ref, gout_ref, y_ref, r_s, decay_s):
    l = q_ref.shape[1]
    h, dk, dv = C_HEADS, C_DK, C_DV
    log_g = [math.log1p(-(2.0 ** (-5.0 - hh))) for hh in range(h)]

    @pl.when(pl.program_id(1) == 0)
    def _():
        r_s[...] = jnp.zeros_like(r_s)
        diff = (lax.broadcasted_iota(jnp.int32, (l, l), 0) - lax.broadcasted_iota(jnp.int32, (l, l), 1)).astype(F32)
        for hh in range(h):
            decay_s[hh] = jnp.where(diff >= 0, jnp.exp(jnp.maximum(diff, 0.0) * log_g[hh]), 0.0)

    t_col = lax.broadcasted_iota(jnp.int32, (l, 1), 0).astype(F32)
    q = q_ref[0]
    k = k_ref[0]
    v = v_ref[0]
    g = g_ref[0].astype(F32)
    for hh in range(h):
        xi = jnp.exp((t_col + 1.0) * log_g[hh])
        zeta = jnp.exp((l - 1.0 - t_col) * log_g[hh])
        qh = q[:, hh * dk:(hh + 1) * dk]
        kh = k[:, hh * dk:(hh + 1) * dk]
        vh = v[:, hh * dv:(hh + 1) * dv]
        r_st = r_s[hh]
        inner = _dot((_dot_nt(qh, kh) * decay_s[hh]).astype(BF16), vh)
        cross = _dot(qh, r_st.astype(BF16)) * xi
        r_s[hh] = math.exp(l * log_g[hh]) * r_st + _dot((kh.astype(F32) * zeta).T.astype(BF16), vh)
        ys = inner + cross
        mu = jnp.mean(ys, axis=1, keepdims=True)
        yc = ys - mu
        yn = yc * lax.rsqrt(jnp.mean(yc * yc, axis=1, keepdims=True) + EPS) * gout_ref[hh:hh + 1, :]
        y_ref[0, :, hh * dv:(hh + 1) * dv] = (yn * g[:, hh * dv:(hh + 1) * dv]).astype(y_ref.dtype)


def retention_core(q, k, v, g, g_out):
    b, s, _ = q.shape
    h, l = C_HEADS, C_CHUNK if s % C_CHUNK == 0 else s
    sk, sv = h * C_DK, h * C_DV
    return pl.pallas_call(
        _retention_kernel, grid=(b, s // l),
        in_specs=[pl.BlockSpec((1, l, sk), lambda i, c: (i, c, 0)), pl.BlockSpec((1, l, sk), lambda i, c: (i, c, 0)),
                  pl.BlockSpec((1, l, sv), lambda i, c: (i, c, 0)), pl.BlockSpec((1, l, sv), lambda i, c: (i, c, 0)),
                  pl.BlockSpec((h, C_DV), lambda i, c: (0, 0))],
        out_specs=pl.BlockSpec((1, l, sv), lambda i, c: (i, c, 0)),
        out_shape=jax.ShapeDtypeStruct((b, s, sv), BF16),
        scratch_shapes=[pltpu.VMEM((h, C_DK, C_DV), F32), pltpu.VMEM((h, l, l), F32)],
        compiler_params=_params("parallel", "arbitrary"), name="retention_core",
    )(q, k, v, g, g_out)


def _retention_epilogue(ys, pos_ref, inv_ref):
    q, k, g, v = ys
    half = C_DK // 2
    ang = pos_ref[...] * inv_ref[...]
    cos, sin = jnp.cos(ang), jnp.sin(ang)

    def rot(x, scale):
        parts = []
        for hh in range(C_HEADS):
            x1, x2 = x[:, hh * C_DK:hh * C_DK + half], x[:, hh * C_DK + half:(hh + 1) * C_DK]
            parts += [(x1 * cos - x2 * sin) * scale, (x2 * cos + x1 * sin) * scale]
        return jnp.concatenate(parts, axis=1)

    return [rot(q, 1.0), rot(k, C_DK ** -0.5), _silu(g), v]


def mixer_c(h3d, positions, g_norm, w_in, g_out, w_out):
    b, s, d = h3d.shape
    sk, sv = C_HEADS * C_DK, C_HEADS * C_DV
    half = C_DK // 2
    w = w_in.astype(BF16)
    x2d = h3d.reshape(b * s, d)
    ws = [w[:, :sk], w[:, sk:2 * sk], w[:, 2 * sk + sv:], w[:, 2 * sk:2 * sk + sv]]
    inv = (C_ROPE_THETA ** (-jnp.arange(half, dtype=F32) * (2.0 / C_DK))).reshape(1, half)
    pos = positions.astype(F32).reshape(b * s, 1)
    q, k, g, v = norm_proj(x2d, g_norm, ws, [BF16, BF16, BF16, BF16], epilogue=_retention_epilogue,
                           row_aux=[pos], const_aux=[inv])
    y = retention_core(q.reshape(b, s, sk), k.reshape(b, s, sk), v.reshape(b, s, sv), g.reshape(b, s, sv), g_out)
    return out_proj(y.reshape(b * s, sv), w_out.astype(BF16), x2d).reshape(b, s, d)


D_SEGMENTS = 8


def _rglru_kernel(xb_ref, gb_ref, cw_ref, cb_ref, wg_ref, bg_ref, lru_ref, y_ref, a_s, b_s):
    rows, c = xb_ref.shape[1], xb_ref.shape[2]
    nseg, bw = D_SEGMENTS, D_BW
    seg = rows // nseg
    x = xb_ref[0].reshape(nseg, seg, c).transpose(1, 0, 2).reshape(rows, c)
    ntail = (D_CONV - 1) * nseg
    tail = x[rows - ntail:, :]
    seg_id = lax.broadcasted_iota(jnp.int32, (ntail, c), 0) & (nseg - 1)
    head = jnp.where(seg_id >= 1, pltpu.roll(tail, 1, 0), 0.0)
    xe = jnp.concatenate([head, x], axis=0)
    xc = x * cw_ref[D_CONV - 1:D_CONV, :] + cb_ref[...]
    for d in range(1, D_CONV):
        xc = xc + xe[ntail - d * nseg:ntail - d * nseg + rows, :] * cw_ref[D_CONV - 1 - d:D_CONV - d, :]
    sp = _softplus(-lru_ref[...])
    for n in range(c // bw):
        cols = slice(n * bw, (n + 1) * bw)
        xcn = xc[:, cols]
        gates = _dot(xcn.astype(BF16), wg_ref[n]) + bg_ref[n]
        r = _sigmoid(gates[:, :bw])
        i = _sigmoid(gates[:, bw:])
        log_a = (-LRU_C * sp[:, cols]) * r
        a = jnp.exp(log_a)
        a_s[:, cols] = a
        var = jnp.tanh(-log_a) * (1.0 + a * a)
        b_s[:, cols] = jnp.where(var > 0.0, var * lax.rsqrt(var), 0.0) * (i * xcn)

    def slab(m):
        return pl.ds(pl.multiple_of(m * nseg, nseg), nseg)

    def totals(m, carry):
        h, acc = carry
        am = a_s[slab(m), :]
        return am * h + b_s[slab(m), :], am * acc

    h_loc, a_tot = lax.fori_loop(0, seg, totals, (jnp.zeros((nseg, c), F32), jnp.ones((nseg, c), F32)), unroll=8)
    carry_in = [jnp.zeros((1, c), F32)]
    for r in range(1, nseg):
        carry_in.append(h_loc[r - 1:r, :] + a_tot[r - 1:r, :] * carry_in[-1])

    def scan(m, h):
        h = a_s[slab(m), :] * h + b_s[slab(m), :]
        b_s[slab(m), :] = h
        return h

    lax.fori_loop(0, seg, scan, jnp.concatenate(carry_in, axis=0), unroll=8)
    h = b_s[...].reshape(seg, nseg, c).transpose(1, 0, 2).reshape(rows, c)
    y_ref[0] = (h * gb_ref[0].astype(F32)).astype(y_ref.dtype)


def rglru_core(gate_br, xb, conv_w, conv_b, w_gates, b_gates, lru):
    b, s, dr = xb.shape
    c = 2 * D_BW
    nb = c // D_BW
    chan = lambda i, n: (i, 0, n)
    return pl.pallas_call(
        _rglru_kernel, grid=(b, dr // c),
        in_specs=[pl.BlockSpec((1, s, c), chan), pl.BlockSpec((1, s, c), chan),
                  pl.BlockSpec((D_CONV, c), lambda i, n: (0, n)), pl.BlockSpec((1, c), lambda i, n: (0, n)),
                  pl.BlockSpec((nb, D_BW, 2 * D_BW), lambda i, n: (n, 0, 0)),
                  pl.BlockSpec((nb, 1, 2 * D_BW), lambda i, n: (n, 0, 0)),
                  pl.BlockSpec((1, c), lambda i, n: (0, n))],
        out_specs=pl.BlockSpec((1, s, c), chan),
        out_shape=jax.ShapeDtypeStruct((b, s, dr), BF16),
        scratch_shapes=[pltpu.VMEM((s, c), F32), pltpu.VMEM((s, c), F32)],
        compiler_params=_params("parallel", "parallel"), name="rglru_core",
    )(xb, gate_br, conv_w, conv_b.reshape(1, dr), w_gates.astype(BF16), b_gates.reshape(D_BLOCKS, 1, 2 * D_BW),
      lru.reshape(1, dr))


def _rglru_epilogue(ys):
    gate_br, xb = ys
    return [jax.nn.gelu(gate_br), xb]


def mixer_d(h3d, g_norm, w_in, conv_w, conv_b, w_gates, b_gates, lru, w_out):
    b, s, d = h3d.shape
    dr = D_BLOCKS * D_BW
    assert s % (8 * D_SEGMENTS) == 0
    w = w_in.astype(BF16)
    x2d = h3d.reshape(b * s, d)
    gate_br, xb = norm_proj(x2d, g_norm, [w[:, :dr], w[:, dr:]], [BF16, F32], epilogue=_rglru_epilogue)
    y = rglru_core(gate_br.reshape(b, s, dr), xb.reshape(b, s, dr), conv_w, conv_b, w_gates, b_gates, lru)
    return out_proj(y.reshape(b * s, dr), w_out.astype(BF16), x2d).reshape(b, s, d)


def ffn_layer(h3d, g_norm, w_in, w_out):
    b, s, d = h3d.shape
    dff = w_out.shape[0]
    w = w_in.astype(BF16)
    return ffn(h3d.reshape(b * s, d), g_norm, w[:, :dff], w[:, dff:], w_out.astype(BF16)).reshape(b, s, d)


def kernel(x, positions, norm_mix, norm_ffn, ffn_w_in, ffn_w_out, a_w_in, a_b_gates, a_g_out, a_w_out, b_w_in, b_g_q, b_g_k, b_w_out, c_w_in, c_g_out, c_w_out, d_w_in, d_conv_w, d_conv_b, d_w_gates, d_b_gates, d_lru, d_w_out):
    h = x
    depth = norm_mix.shape[0]
    for layer in range(depth):
        mixer, r = layer % 4, layer // 4
        if mixer == 0:
            h = mixer_a(h, norm_mix[layer], a_w_in[r], a_b_gates[r], a_g_out[r], a_w_out[r])
        elif mixer == 1:
            h = mixer_b(h, positions, norm_mix[layer], b_w_in[r], b_g_q[r], b_g_k[r], b_w_out[r])
        elif mixer == 2:
            h = mixer_c(h, positions, norm_mix[layer], c_w_in[r], c_g_out[r], c_w_out[r])
        else:
            h = mixer_d(h, norm_mix[layer], d_w_in[r], d_conv_w[r], d_conv_b[r], d_w_gates[r], d_b_gates[r],
                        d_lru[r], d_w_out[r])
        h = ffn_layer(h, norm_ffn[layer], ffn_w_in[layer], ffn_w_out[layer])
    return h
```

```python
import functools
import math

import jax
import jax.numpy as jnp
from jax import lax
from jax.experimental import pallas as pl
from jax.experimental.pallas import tpu as pltpu

F32 = jnp.float32
BF16 = jnp.bfloat16
EPS = 1e-6
NEG_INF = float("-inf")
HIGHEST = lax.Precision.HIGHEST

LANES = 128
MXU_DIM = 256
VMEM_LIMIT_BYTES = 56 * 1024 * 1024

A_HEADS, A_DQK, A_DV = 4, 128, 256
A_CHUNK = 256
B_HEADS, B_DH, B_BLOCK, B_TOPK, B_QBLOCK = 8, 128, 256, 3, 128
B_ROPE_THETA, B_ROPE_DIMS = 500000.0, 32
C_HEADS, C_DK, C_DV, C_CHUNK = 4, 256, 512, 256
C_ROPE_THETA = 10000.0
D_BLOCKS, D_BW, D_CONV, LRU_C = 4, 256, 4, 8.0


def _params(*semantics):
    return pltpu.CompilerParams(dimension_semantics=semantics, vmem_limit_bytes=VMEM_LIMIT_BYTES)


def _row_tile(t, want=512):
    return want if t % want == 0 else t


def _sigmoid(x):
    return 0.5 * jnp.tanh(0.5 * x) + 0.5


def _silu(x):
    return x * _sigmoid(x)


def _softplus(x):
    return jnp.maximum(x, 0.0) + jnp.log1p(jnp.exp(-jnp.abs(x)))


def _rms(x, g):
    return x * lax.rsqrt(jnp.mean(x * x, axis=-1, keepdims=True) + EPS) * g


def _dot(a, b):
    return jnp.dot(a, b, preferred_element_type=F32)


def _dot_nt(a, b):
    return lax.dot_general(a, b, (((1,), (1,)), ((), ())), preferred_element_type=F32)


def _norm_proj_kernel(x_ref, g_ref, *refs, n, epilogue):
    w_refs, o_refs, aux_refs = refs[:n], refs[len(refs) - n:], refs[n:len(refs) - n]
    u = _rms(x_ref[...], g_ref[...]).astype(BF16)
    ys = [_dot(u, w_ref[...]) for w_ref in w_refs]
    if epilogue is not None:
        ys = epilogue(ys, *aux_refs)
    for y, o_ref in zip(ys, o_refs):
        o_ref[...] = y.astype(o_ref.dtype)


def norm_proj(x2d, g, weights, out_dtypes, epilogue=None, row_aux=(), const_aux=()):
    t, d = x2d.shape
    tm = _row_tile(t)
    in_specs = [pl.BlockSpec((tm, d), lambda i: (i, 0)), pl.BlockSpec((1, d), lambda i: (0, 0))]
    in_specs += [pl.BlockSpec(w.shape, lambda i: (0, 0)) for w in weights]
    in_specs += [pl.BlockSpec((tm, a.shape[1]), lambda i: (i, 0)) for a in row_aux]
    in_specs += [pl.BlockSpec(a.shape, lambda i: (0, 0)) for a in const_aux]
    out_specs = [pl.BlockSpec((tm, w.shape[1]), lambda i: (i, 0)) for w in weights]
    out_shape = [jax.ShapeDtypeStruct((t, w.shape[1]), dt) for w, dt in zip(weights, out_dtypes)]
    return pl.pallas_call(
        functools.partial(_norm_proj_kernel, n=len(weights), epilogue=epilogue), grid=(t // tm,),
        in_specs=in_specs, out_specs=out_specs, out_shape=out_shape,
        compiler_params=_params("parallel"), name="norm_proj",
    )(x2d, g.reshape(1, d), *weights, *row_aux, *const_aux)


def _ffn_kernel(y_ref, wm_ref, x_ref, g_ref, wg_ref, wu_ref, wo_ref, o_ref, *, n_chunks):
    x = x_ref[...] + _dot(y_ref[...], wm_ref[...])
    u = _rms(x, g_ref[...]).astype(BF16)
    fc = wg_ref.shape[1] // n_chunks
    acc = x
    for c in range(n_chunks):
        gt = _dot(u, wg_ref[:, c * fc:(c + 1) * fc])
        up = _dot(u, wu_ref[:, c * fc:(c + 1) * fc])
        act = (_silu(gt) * up).astype(BF16)
        acc = acc + _dot(act, wo_ref[c * fc:(c + 1) * fc, :])
    o_ref[...] = acc


def _resident(shape):
    return pl.BlockSpec(shape, lambda i: (0,) * len(shape), pipeline_mode=pl.Buffered(1))


def mix_out_ffn(y2d, w_mix, x2d, g, w_gate, w_up, w_out):
    t, d = x2d.shape
    k, dff = w_mix.shape[0], w_gate.shape[1]
    tm = _row_tile(t)
    n_chunks = dff // MXU_DIM if dff % MXU_DIM == 0 else 1
    return pl.pallas_call(
        functools.partial(_ffn_kernel, n_chunks=n_chunks), grid=(t // tm,),
        in_specs=[pl.BlockSpec((tm, k), lambda i: (i, 0)), _resident((k, d)),
                  pl.BlockSpec((tm, d), lambda i: (i, 0)), _resident((1, d)),
                  _resident((d, dff)), _resident((d, dff)), _resident((dff, d))],
        out_specs=pl.BlockSpec((tm, d), lambda i: (i, 0)),
        out_shape=jax.ShapeDtypeStruct((t, d), F32),
        compiler_params=_params("parallel"), name="mix_out_ffn",
    )(y2d, w_mix, x2d, g.reshape(1, d), w_gate, w_up, w_out)


def _mlstm_kernel(q_ref, k_ref, v_ref, o_ref, gc_ref, gr_ref, bc_ref, br_ref, gout_ref, y_ref,
                  ct_s, n_s, m_s):
    l = q_ref.shape[1]
    h, dqk, dv = A_HEADS, A_DQK, A_DV
    scale = dqk ** -0.5

    @pl.when(pl.program_id(1) == 0)
    def _():
        ct_s[...] = jnp.zeros_like(ct_s)
        n_s[...] = jnp.zeros_like(n_s)
        m_s[...] = jnp.zeros_like(m_s)

    gc = gc_ref[0] + bc_ref[...]
    gr = gr_ref[0, 0] + br_ref[...]
    lf_c = jnp.minimum(gc, 0.0) - jnp.log1p(jnp.exp(-jnp.abs(gc)))
    lf_r = jnp.minimum(gr, 0.0) - jnp.log1p(jnp.exp(-jnp.abs(gr)))
    row = lax.broadcasted_iota(jnp.int32, (l, l), 0)
    col = lax.broadcasted_iota(jnp.int32, (l, l), 1)
    causal = row >= col
    bcum_c = jnp.dot(causal.astype(F32), lf_c, precision=HIGHEST, preferred_element_type=F32)
    bcum_r = jnp.dot(lf_r, (row <= col).astype(F32), precision=HIGHEST, preferred_element_type=F32)

    q = q_ref[0]
    k = k_ref[0]
    v = v_ref[0]
    og = o_ref[0].astype(F32)
    for hh in range(h):
        bc = bcum_c[:, h + hh:h + hh + 1]
        br = bcum_r[h + hh:h + hh + 1, :]
        ic = gc[:, hh:hh + 1]
        ir = gr[hh:hh + 1, :]
        btot = bc[l - 1:l, :]
        m_st = m_s[hh][:, :1]
        qh = q[:, hh * dqk:(hh + 1) * dqk]
        kh = k[:, hh * dqk:(hh + 1) * dqk]
        vh = v[:, hh * dv:(hh + 1) * dv]
        ct = ct_s[hh]
        nst = n_s[hh]

        dmat = jnp.where(causal, bc - br + ir, NEG_INF)
        inter = bc + m_st
        m_t = jnp.maximum(inter, jnp.max(dmat, axis=1, keepdims=True))
        w_intra = jnp.exp(dmat - m_t) * scale
        w_inter = jnp.exp(inter - m_t) * scale
        qk = _dot_nt(qh, kh) * w_intra
        num = _dot(qk.astype(BF16), vh) + w_inter * _dot(qh, ct.astype(BF16))
        den = jnp.sum(qk, axis=1, keepdims=True) + w_inter * jnp.sum(qh.astype(F32) * nst, axis=1, keepdims=True)
        hout = num / jnp.maximum(jnp.abs(den), jnp.exp(-m_t))

        dec = btot - bc + ic
        m_new = jnp.maximum(btot + m_st, jnp.max(dec, axis=0, keepdims=True))
        ws = jnp.exp(dec - m_new)
        wc = jnp.exp(btot + m_st - m_new)
        kw = kh.astype(F32) * ws
        ct_s[hh] = wc * ct + _dot(kw.T.astype(BF16), vh)
        n_s[hh] = wc * nst + jnp.sum(kw, axis=0, keepdims=True)
        m_s[hh] = jnp.broadcast_to(m_new, (1, LANES))

        hn = _rms(hout, gout_ref[hh:hh + 1, :])
        y_ref[0, :, hh * dv:(hh + 1) * dv] = (hn * og[:, hh * dv:(hh + 1) * dv]).astype(y_ref.dtype)


def mlstm_core(q, k, v, o, gates, b_gates, g_out):
    b, s, _ = q.shape
    h, l = A_HEADS, A_CHUNK if s % A_CHUNK == 0 else s
    nc = s // l
    gates_row = gates[..., :2 * h].reshape(b, nc, l, 2 * h).transpose(0, 1, 3, 2)
    bc = jnp.zeros((1, LANES), F32).at[0, :2 * h].set(b_gates)
    br = b_gates.reshape(2 * h, 1)
    sq, sv = h * A_DQK, h * A_DV
    return pl.pallas_call(
        _mlstm_kernel, grid=(b, nc),
        in_specs=[pl.BlockSpec((1, l, sq), lambda i, c: (i, c, 0)), pl.BlockSpec((1, l, sq), lambda i, c: (i, c, 0)),
                  pl.BlockSpec((1, l, sv), lambda i, c: (i, c, 0)), pl.BlockSpec((1, l, sv), lambda i, c: (i, c, 0)),
                  pl.BlockSpec((1, l, LANES), lambda i, c: (i, c, 0)),
                  pl.BlockSpec((1, 1, 2 * h, l), lambda i, c: (i, c, 0, 0)),
                  pl.BlockSpec((1, LANES), lambda i, c: (0, 0)), pl.BlockSpec((2 * h, 1), lambda i, c: (0, 0)),
                  pl.BlockSpec((h, A_DV), lambda i, c: (0, 0))],
        out_specs=pl.BlockSpec((1, l, sv), lambda i, c: (i, c, 0)),
        out_shape=jax.ShapeDtypeStruct((b, s, sv), BF16),
        scratch_shapes=[pltpu.VMEM((h, A_DQK, A_DV), F32), pltpu.VMEM((h, 1, A_DQK), F32),
                        pltpu.VMEM((h, 1, LANES), F32)],
        compiler_params=_params("parallel", "arbitrary"), name="mlstm_core",
    )(q, k, v, o, gates, gates_row, bc, br, g_out)


def _mlstm_epilogue(ys):
    o, q, k, v, gates = ys
    return [_sigmoid(o), q, k, v, gates]


def mixer_a(h3d, g_norm, w_in, b_gates, g_out):
    b, s, d = h3d.shape
    hh = A_HEADS
    sq, sv = hh * A_DQK, hh * A_DV
    w = w_in.astype(BF16)
    w_gates = jnp.zeros((d, LANES), BF16).at[:, :2 * hh].set(w[:, 2 * sq + 2 * sv:])
    ws = [w[:, 2 * sq + sv:2 * sq + 2 * sv], w[:, :sq], w[:, sq:2 * sq], w[:, 2 * sq:2 * sq + sv], w_gates]
    x2d = h3d.reshape(b * s, d)
    o, q, k, v, gates = norm_proj(x2d, g_norm, ws, [BF16, BF16, BF16, BF16, F32], epilogue=_mlstm_epilogue)
    y = mlstm_core(q.reshape(b, s, sq), k.reshape(b, s, sq), v.reshape(b, s, sv), o.reshape(b, s, sv),
                   gates.reshape(b, s, LANES), b_gates, g_out)
    return y.reshape(b * s, sv)


def _moba_kernel(q_ref, k_ref, v_ref, o_ref, km_s, *, topk):
    s, dh = q_ref.shape[1], q_ref.shape[2]
    nblk = s // B_BLOCK
    qn = q_ref[0]
    kn = k_ref[0]
    qb16 = qn.astype(BF16)
    kb16 = kn.astype(BF16)
    km_s[...] = jnp.zeros_like(km_s)
    for n in range(nblk):
        km_s[n:n + 1, :] = jnp.mean(kn[n * B_BLOCK:(n + 1) * B_BLOCK, :], axis=0, keepdims=True)

    nb8 = -(-nblk // 8) * 8
    blk = lax.broadcasted_iota(jnp.int32, (nb8, B_BLOCK), 0)
    tri = (lax.broadcasted_iota(jnp.int32, (B_BLOCK, B_BLOCK), 0)
           >= lax.broadcasted_iota(jnp.int32, (B_BLOCK, B_BLOCK), 1))
    for j in range(nblk):
        rows = slice(j * B_BLOCK, (j + 1) * B_BLOCK)
        qj = qb16[rows]
        select = j > topk
        if select:
            gate = lax.dot_general(km_s[:nb8, :], qn[rows], (((1,), (1,)), ((), ())), precision=HIGHEST,
                                   preferred_element_type=F32)
            gate = jnp.where(blk < j, gate, NEG_INF)
            rank = jnp.zeros(gate.shape, F32)
            for m in range(j):
                gm = gate[m:m + 1, :]
                rank = rank + jnp.where((gm > gate) | ((gm == gate) & (blk > m)), 1.0, 0.0)
            bias = jnp.where(rank < topk, 0.0, NEG_INF)
            bias = jnp.concatenate([bias, jnp.zeros((LANES - nb8, B_BLOCK), F32)], axis=0).T
        parts = []
        for n in range(j):
            sc = _dot_nt(qj, kb16[n * B_BLOCK:(n + 1) * B_BLOCK])
            parts.append(sc + bias[:, n:n + 1] if select else sc)
        parts.append(jnp.where(tri, _dot_nt(qj, kb16[rows]), NEG_INF))
        sc = jnp.concatenate(parts, axis=1) if j else parts[0]
        p = jnp.exp(sc - jnp.max(sc, axis=1, keepdims=True))
        out = _dot(p.astype(BF16), v_ref[0, :(j + 1) * B_BLOCK, :]) / jnp.sum(p, axis=1, keepdims=True)
        o_ref[0, rows, :] = out.astype(o_ref.dtype)


def moba_core(q, k, v):
    b, s, hd = q.shape
    h, dh = B_HEADS, B_DH
    assert s % B_BLOCK == 0 and s // B_BLOCK <= LANES
    topk = min(B_TOPK, s // B_BLOCK - 1)
    head_spec = pl.BlockSpec((1, s, dh), lambda i, hh: (i, 0, hh))
    return pl.pallas_call(
        functools.partial(_moba_kernel, topk=topk), grid=(b, h),
        in_specs=[head_spec, head_spec, head_spec],
        out_specs=head_spec,
        out_shape=jax.ShapeDtypeStruct((b, s, hd), BF16),
        scratch_shapes=[pltpu.VMEM((LANES, dh), F32)],
        compiler_params=_params("parallel", "parallel"), name="moba_core",
    )(q, k, v)


def _moba_epilogue(ys, pos_ref, inv_ref, gq_ref, gk_ref):
    q, k, v = ys
    dh, half = B_DH, B_ROPE_DIMS // 2
    ang = pos_ref[...] * inv_ref[...]
    first = lax.broadcasted_iota(jnp.int32, ang.shape, 1) < half
    cos = jnp.cos(ang)
    sin = jnp.where(first, -jnp.sin(ang), jnp.sin(ang))

    def prep(x, g_ref, scale):
        parts = []
        for hh in range(B_HEADS):
            xh = _rms(x[:, hh * dh:(hh + 1) * dh], g_ref[...])
            partner = jnp.where(first, pltpu.roll(xh, dh - half, 1), pltpu.roll(xh, half, 1))
            parts.append((xh * cos + partner * sin) * scale)
        return jnp.concatenate(parts, axis=1)

    return [prep(q, gq_ref, dh ** -0.5), prep(k, gk_ref, 1.0), v]


def mixer_b(h3d, positions, g_norm, w_in, g_q, g_k):
    b, s, d = h3d.shape
    hd, dh, half = B_HEADS * B_DH, B_DH, B_ROPE_DIMS // 2
    w = w_in.astype(BF16)
    x2d = h3d.reshape(b * s, d)
    inv = B_ROPE_THETA ** (-jnp.arange(half, dtype=F32) * (2.0 / B_ROPE_DIMS))
    inv_row = jnp.zeros((1, dh), F32).at[0, :half].set(inv).at[0, half:2 * half].set(inv)
    pos = positions.astype(F32).reshape(b * s, 1)
    q, k, v = norm_proj(x2d, g_norm, [w[:, :hd], w[:, hd:2 * hd], w[:, 2 * hd:]], [F32, F32, BF16],
                        epilogue=_moba_epilogue, row_aux=[pos], const_aux=[inv_row, g_q.reshape(1, dh), g_k.reshape(1, dh)])
    y = moba_core(q.reshape(b, s, hd), k.reshape(b, s, hd), v.reshape(b, s, hd))
    return y.reshape(b * s, hd)


def _retention_kernel(q_ref, k_ref, v_ref, g_ref, gout_ref, y_ref, r_s, decay_s):
    l = q_ref.shape[1]
    h, dk, dv = C_HEADS, C_DK, C_DV
    log_g = [math.log1p(-(2.0 ** (-5.0 - hh))) for hh in range(h)]

    @pl.when(pl.program_id(1) == 0)
    def _():
        r_s[...] = jnp.zeros_like(r_s)
        diff = (lax.broadcasted_iota(jnp.int32, (l, l), 0) - lax.broadcasted_iota(jnp.int32, (l, l), 1)).astype(F32)
        for hh in range(h):
            decay_s[hh] = jnp.where(diff >= 0, jnp.exp(jnp.maximum(diff, 0.0) * log_g[hh]), 0.0)

    t_col = lax.broadcasted_iota(jnp.int32, (l, 1), 0).astype(F32)
    q = q_ref[0]
    k = k_ref[0]
    v = v_ref[0]
    g = g_ref[0].astype(F32)
    for hh in range(h):
        xi = jnp.exp((t_col + 1.0) * log_g[hh])
        zeta = jnp.exp((l - 1.0 - t_col) * log_g[hh])
        qh = q[:, hh * dk:(hh + 1) * dk]
        kh = k[:, hh * dk:(hh + 1) * dk]
        vh = v[:, hh * dv:(hh + 1) * dv]
        r_st = r_s[hh]
        inner = _dot((_dot_nt(qh, kh) * decay_s[hh]).astype(BF16), vh)
        cross = _dot(qh, r_st.astype(BF16)) * xi
        r_s[hh] = math.exp(l * log_g[hh]) * r_st + _dot((kh.astype(F32) * zeta).T.astype(BF16), vh)
        ys = inner + cross
        mu = jnp.mean(ys, axis=1, keepdims=True)
        yc = ys - mu
        yn = yc * lax.rsqrt(jnp.mean(yc * yc, axis=1, keepdims=True) + EPS) * gout_ref[hh:hh + 1, :]
        y_ref[0, :, hh * dv:(hh + 1) * dv] = (yn * g[:, hh * dv:(hh + 1) * dv]).astype(y_ref.dtype)


def retention_core(q, k, v, g, g_out):
    b, s, _ = q.shape
    h, l = C_HEADS, C_CHUNK if s % C_CHUNK == 0 else s
    sk, sv = h * C_DK, h * C_DV
    return pl.pallas_call(
        _retention_kernel, grid=(b, s // l),
        in_specs=[pl.BlockSpec((1, l, sk), lambda i, c: (i, c, 0)), pl.BlockSpec((1, l, sk), lambda i, c: (i, c, 0)),
                  pl.BlockSpec((1, l, sv), lambda i, c: (i, c, 0)), pl.BlockSpec((1, l, sv), lambda i, c: (i, c, 0)),
                  pl.BlockSpec((h, C_DV), lambda i, c: (0, 0))],
        out_specs=pl.BlockSpec((1, l, sv), lambda i, c: (i, c, 0)),
        out_shape=jax.ShapeDtypeStruct((b, s, sv), BF16),
        scratch_shapes=[pltpu.VMEM((h, C_DK, C_DV), F32), pltpu.VMEM((h, l, l), F32)],
        compiler_params=_params("parallel", "arbitrary"), name="retention_core",
    )(q, k, v, g, g_out)


def _retention_epilogue(ys, pos_ref, inv_ref):
    q, k, g, v = ys
    half = C_DK // 2
    ang = pos_ref[...] * inv_ref[...]
    cos, sin = jnp.cos(ang), jnp.sin(ang)

    def rot(x, scale):
        parts = []
        for hh in range(C_HEADS):
            x1, x2 = x[:, hh * C_DK:hh * C_DK + half], x[:, hh * C_DK + half:(hh + 1) * C_DK]
            parts += [(x1 * cos - x2 * sin) * scale, (x2 * cos + x1 * sin) * scale]
        return jnp.concatenate(parts, axis=1)

    return [rot(q, 1.0), rot(k, C_DK ** -0.5), _silu(g), v]


def mixer_c(h3d, positions, g_norm, w_in, g_out):
    b, s, d = h3d.shape
    sk, sv = C_HEADS * C_DK, C_HEADS * C_DV
    half = C_DK // 2
    w = w_in.astype(BF16)
    x2d = h3d.reshape(b * s, d)
    ws = [w[:, :sk], w[:, sk:2 * sk], w[:, 2 * sk + sv:], w[:, 2 * sk:2 * sk + sv]]
    inv = (C_ROPE_THETA ** (-jnp.arange(half, dtype=F32) * (2.0 / C_DK))).reshape(1, half)
    pos = positions.astype(F32).reshape(b * s, 1)
    q, k, g, v = norm_proj(x2d, g_norm, ws, [BF16, BF16, BF16, BF16], epilogue=_retention_epilogue,
                           row_aux=[pos], const_aux=[inv])
    y = retention_core(q.reshape(b, s, sk), k.reshape(b, s, sk), v.reshape(b, s, sv), g.reshape(b, s, sv), g_out)
    return y.reshape(b * s, sv)


D_SEGMENTS = 8


def _rglru_kernel(xb_ref, gb_ref, cw_ref, cb_ref, wg_ref, bg_ref, lru_ref, y_ref, a_s, b_s):
    rows, c = xb_ref.shape[1], xb_ref.shape[2]
    nseg, bw = D_SEGMENTS, D_BW
    seg = rows // nseg
    x = xb_ref[0].reshape(nseg, seg, c).transpose(1, 0, 2).reshape(rows, c)
    ntail = (D_CONV - 1) * nseg
    tail = x[rows - ntail:, :]
    seg_id = lax.broadcasted_iota(jnp.int32, (ntail, c), 0) & (nseg - 1)
    head = jnp.where(seg_id >= 1, pltpu.roll(tail, 1, 0), 0.0)
    xe = jnp.concatenate([head, x], axis=0)
    xc = x * cw_ref[D_CONV - 1:D_CONV, :] + cb_ref[...]
    for d in range(1, D_CONV):
        xc = xc + xe[ntail - d * nseg:ntail - d * nseg + rows, :] * cw_ref[D_CONV - 1 - d:D_CONV - d, :]
    sp = _softplus(-lru_ref[...])
    for n in range(c // bw):
        cols = slice(n * bw, (n + 1) * bw)
        xcn = xc[:, cols]
        gates = _dot(xcn.astype(BF16), wg_ref[n]) + bg_ref[n]
        r = _sigmoid(gates[:, :bw])
        i = _sigmoid(gates[:, bw:])
        log_a = (-LRU_C * sp[:, cols]) * r
        a = jnp.exp(log_a)
        a_s[:, cols] = a
        var = jnp.tanh(-log_a) * (1.0 + a * a)
        b_s[:, cols] = jnp.where(var > 0.0, var * lax.rsqrt(var), 0.0) * (i * xcn)

    def slab(m):
        return pl.ds(pl.multiple_of(m * nseg, nseg), nseg)

    def totals(m, carry):
        h, acc = carry
        am = a_s[slab(m), :]
        return am * h + b_s[slab(m), :], am * acc

    h_loc, a_tot = lax.fori_loop(0, seg, totals, (jnp.zeros((nseg, c), F32), jnp.ones((nseg, c), F32)), unroll=8)
    carry_in = [jnp.zeros((1, c), F32)]
    for r in range(1, nseg):
        carry_in.append(h_loc[r - 1:r, :] + a_tot[r - 1:r, :] * carry_in[-1])

    def scan(m, h):
        h = a_s[slab(m), :] * h + b_s[slab(m), :]
        b_s[slab(m), :] = h
        return h

    lax.fori_loop(0, seg, scan, jnp.concatenate(carry_in, axis=0), unroll=8)
    h = b_s[...].reshape(seg, nseg, c).transpose(1, 0, 2).reshape(rows, c)
    y_ref[0] = (h * gb_ref[0].astype(F32)).astype(y_ref.dtype)


def rglru_core(gate_br, xb, conv_w, conv_b, w_gates, b_gates, lru):
    b, s, dr = xb.shape
    c = 2 * D_BW
    nb = c // D_BW
    chan = lambda i, n: (i, 0, n)
    return pl.pallas_call(
        _rglru_kernel, grid=(b, dr // c),
        in_specs=[pl.BlockSpec((1, s, c), chan), pl.BlockSpec((1, s, c), chan),
                  pl.BlockSpec((D_CONV, c), lambda i, n: (0, n)), pl.BlockSpec((1, c), lambda i, n: (0, n)),
                  pl.BlockSpec((nb, D_BW, 2 * D_BW), lambda i, n: (n, 0, 0)),
                  pl.BlockSpec((nb, 1, 2 * D_BW), lambda i, n: (n, 0, 0)),
                  pl.BlockSpec((1, c), lambda i, n: (0, n))],
        out_specs=pl.BlockSpec((1, s, c), chan),
        out_shape=jax.ShapeDtypeStruct((b, s, dr), BF16),
        scratch_shapes=[pltpu.VMEM((s, c), F32), pltpu.VMEM((s, c), F32)],
        compiler_params=_params("parallel", "parallel"), name="rglru_core",
    )(xb, gate_br, conv_w, conv_b.reshape(1, dr), w_gates.astype(BF16), b_gates.reshape(D_BLOCKS, 1, 2 * D_BW),
      lru.reshape(1, dr))


def _rglru_epilogue(ys):
    gate_br, xb = ys
    return [jax.nn.gelu(gate_br), xb]


def mixer_d(h3d, g_norm, w_in, conv_w, conv_b, w_gates, b_gates, lru):
    b, s, d = h3d.shape
    dr = D_BLOCKS * D_BW
    assert s % (8 * D_SEGMENTS) == 0
    w = w_in.astype(BF16)
    x2d = h3d.reshape(b * s, d)
    gate_br, xb = norm_proj(x2d, g_norm, [w[:, :dr], w[:, dr:]], [BF16, F32], epilogue=_rglru_epilogue)
    y = rglru_core(gate_br.reshape(b, s, dr), xb.reshape(b, s, dr), conv_w, conv_b, w_gates, b_gates, lru)
    return y.reshape(b * s, dr)


def ffn_layer(y2d, w_mix, h3d, g_norm, w_in, w_out):
    b, s, d = h3d.shape
    dff = w_out.shape[0]
    w = w_in.astype(BF16)
    return mix_out_ffn(y2d, w_mix.astype(BF16), h3d.reshape(b * s, d), g_norm, w[:, :dff], w[:, dff:],
                       w_out.astype(BF16)).reshape(b, s, d)


def kernel(x, positions, norm_mix, norm_ffn, ffn_w_in, ffn_w_out, a_w_in, a_b_gates, a_g_out, a_w_out, b_w_in, b_g_q, b_g_k, b_w_out, c_w_in, c_g_out, c_w_out, d_w_in, d_conv_w, d_conv_b, d_w_gates, d_b_gates, d_lru, d_w_out):
    h = x
    depth = norm_mix.shape[0]
    for layer in range(depth):
        mixer, r = layer % 4, layer // 4
        if mixer == 0:
            y, w_mix = mixer_a(h, norm_mix[layer], a_w_in[r], a_b_gates[r], a_g_out[r]), a_w_out[r]
        elif mixer == 1:
            y, w_mix = mixer_b(h, positions, norm_mix[layer], b_w_in[r], b_g_q[r], b_g_k[r]), b_w_out[r]
        elif mixer == 2:
            y, w_mix = mixer_c(h, positions, norm_mix[layer], c_w_in[r], c_g_out[r]), c_w_out[r]
        else:
            y, w_mix = mixer_d(h, norm_mix[layer], d_w_in[r], d_conv_w[r], d_conv_b[r], d_w_gates[r], d_b_gates[r],
                               d_lru[r]), d_w_out[r]
        h = ffn_layer(y, w_mix, h, norm_ffn[layer], ffn_w_in[layer], ffn_w_out[layer])
    return h
```

```python
import functools
import math

import jax
import jax.numpy as jnp
from jax import lax
from jax.experimental import pallas as pl
from jax.experimental.pallas import tpu as pltpu

F32 = jnp.float32
BF16 = jnp.bfloat16
EPS = 1e-6
NEG_INF = float("-inf")
HIGHEST = lax.Precision.HIGHEST

LANES = 128
MXU_DIM = 256
VMEM_LIMIT_BYTES = 56 * 1024 * 1024

A_HEADS, A_DQK, A_DV = 4, 128, 256
A_CHUNK = 256
B_HEADS, B_DH, B_BLOCK, B_TOPK, B_QBLOCK = 8, 128, 256, 3, 128
B_ROPE_THETA, B_ROPE_DIMS = 500000.0, 32
C_HEADS, C_DK, C_DV, C_CHUNK = 4, 256, 512, 256
C_ROPE_THETA = 10000.0
D_BLOCKS, D_BW, D_CONV, LRU_C = 4, 256, 4, 8.0


def _params(*semantics):
    return pltpu.CompilerParams(dimension_semantics=semantics, vmem_limit_bytes=VMEM_LIMIT_BYTES)


def _row_tile(t, want=512):
    return want if t % want == 0 else t


def _sigmoid(x):
    return 0.5 * jnp.tanh(0.5 * x) + 0.5


def _silu(x):
    return x * _sigmoid(x)


def _softplus(x):
    return jnp.maximum(x, 0.0) + jnp.log1p(jnp.exp(-jnp.abs(x)))


def _rms(x, g):
    return x * lax.rsqrt(jnp.mean(x * x, axis=-1, keepdims=True) + EPS) * g


def _dot(a, b):
    return jnp.dot(a, b, preferred_element_type=F32)


def _dot_nt(a, b):
    return lax.dot_general(a, b, (((1,), (1,)), ((), ())), preferred_element_type=F32)


def _norm_proj_kernel(x_ref, g_ref, w_ref, *refs, segments, epilogue):
    n = len(segments)
    aux_refs, o_refs = refs[:len(refs) - n], refs[len(refs) - n:]
    u = _rms(x_ref[...], g_ref[...]).astype(BF16)
    ys = [_dot(u, w_ref[:, start:start + width]) for start, width in segments]
    if epilogue is not None:
        ys = epilogue(ys, *aux_refs)
    for y, o_ref in zip(ys, o_refs):
        o_ref[...] = y.astype(o_ref.dtype)


def _resident(shape):
    return pl.BlockSpec(shape, lambda i: (0,) * len(shape), pipeline_mode=pl.Buffered(1))


def norm_proj(x2d, g, w, segments, out_dtypes, epilogue=None, row_aux=(), const_aux=()):
    t, d = x2d.shape
    tm = _row_tile(t)
    in_specs = [pl.BlockSpec((tm, d), lambda i: (i, 0)), _resident((1, d)), _resident(w.shape)]
    in_specs += [pl.BlockSpec((tm, a.shape[1]), lambda i: (i, 0)) for a in row_aux]
    in_specs += [_resident(a.shape) for a in const_aux]
    out_specs = [pl.BlockSpec((tm, width), lambda i: (i, 0)) for _, width in segments]
    out_shape = [jax.ShapeDtypeStruct((t, width), dt) for (_, width), dt in zip(segments, out_dtypes)]
    return pl.pallas_call(
        functools.partial(_norm_proj_kernel, segments=tuple(segments), epilogue=epilogue), grid=(t // tm,),
        in_specs=in_specs, out_specs=out_specs, out_shape=out_shape,
        compiler_params=_params("parallel"), name="norm_proj",
    )(x2d, g.reshape(1, d), w, *row_aux, *const_aux)


def _ffn_kernel(y_ref, wm_ref, x_ref, g_ref, wi_ref, wo_ref, o_ref, *, n_chunks):
    x = x_ref[...] + _dot(y_ref[...], wm_ref[...])
    u = _rms(x, g_ref[...]).astype(BF16)
    dff = wo_ref.shape[0]
    fc = dff // n_chunks
    acc = x
    for c in range(n_chunks):
        gt = _dot(u, wi_ref[:, c * fc:(c + 1) * fc])
        up = _dot(u, wi_ref[:, dff + c * fc:dff + (c + 1) * fc])
        act = (_silu(gt) * up).astype(BF16)
        acc = acc + _dot(act, wo_ref[c * fc:(c + 1) * fc, :])
    o_ref[...] = acc


def mix_out_ffn(y2d, w_mix, x2d, g, w_in, w_out):
    t, d = x2d.shape
    k, dff = w_mix.shape[0], w_out.shape[0]
    tm = _row_tile(t)
    n_chunks = dff // MXU_DIM if dff % MXU_DIM == 0 else 1
    return pl.pallas_call(
        functools.partial(_ffn_kernel, n_chunks=n_chunks), grid=(t // tm,),
        in_specs=[pl.BlockSpec((tm, k), lambda i: (i, 0)), _resident((k, d)),
                  pl.BlockSpec((tm, d), lambda i: (i, 0)), _resident((1, d)),
                  _resident((d, 2 * dff)), _resident((dff, d))],
        out_specs=pl.BlockSpec((tm, d), lambda i: (i, 0)),
        out_shape=jax.ShapeDtypeStruct((t, d), F32),
        compiler_params=_params("parallel"), name="mix_out_ffn",
    )(y2d, w_mix, x2d, g.reshape(1, d), w_in, w_out)


def _mlstm_kernel(q_ref, k_ref, v_ref, o_ref, gc_ref, gr_ref, bc_ref, br_ref, gout_ref, y_ref,
                  ct_s, n_s, m_s):
    l = q_ref.shape[1]
    h, dqk, dv = A_HEADS, A_DQK, A_DV
    scale = dqk ** -0.5

    @pl.when(pl.program_id(1) == 0)
    def _():
        ct_s[...] = jnp.zeros_like(ct_s)
        n_s[...] = jnp.zeros_like(n_s)
        m_s[...] = jnp.zeros_like(m_s)

    gc = gc_ref[0] + bc_ref[...]
    gr = gr_ref[0, 0] + br_ref[...]
    lf_c = jnp.minimum(gc, 0.0) - jnp.log1p(jnp.exp(-jnp.abs(gc)))
    lf_r = jnp.minimum(gr, 0.0) - jnp.log1p(jnp.exp(-jnp.abs(gr)))
    row = lax.broadcasted_iota(jnp.int32, (l, l), 0)
    col = lax.broadcasted_iota(jnp.int32, (l, l), 1)
    causal = row >= col
    bcum_c = jnp.dot(causal.astype(F32), lf_c, precision=HIGHEST, preferred_element_type=F32)
    bcum_r = jnp.dot(lf_r, (row <= col).astype(F32), precision=HIGHEST, preferred_element_type=F32)

    def wide(x, width):
        return jnp.concatenate([x] * (width // LANES), axis=1)

    ones_l = jnp.ones((l, LANES), BF16)
    ones_dv = jnp.ones((dv, LANES), BF16)
    q = q_ref[0]
    k = k_ref[0]
    v = v_ref[0]
    og = o_ref[0].astype(F32)
    for hh in range(h):
        bc = jnp.broadcast_to(bcum_c[:, h + hh:h + hh + 1], (l, LANES))
        ic = jnp.broadcast_to(gc[:, hh:hh + 1], (l, LANES))
        br = bcum_r[h + hh:h + hh + 1, :]
        ir = gr[hh:hh + 1, :]
        btot = bc[l - 1:l, :]
        m_st = m_s[hh]
        qh = q[:, hh * dqk:(hh + 1) * dqk]
        kh = k[:, hh * dqk:(hh + 1) * dqk]
        vh = v[:, hh * dv:(hh + 1) * dv]
        ct = ct_s[hh]
        nmat = n_s[hh]

        dmat = jnp.where(causal, wide(bc, l) - br + ir, NEG_INF)
        inter = bc + m_st
        m_t = jnp.maximum(inter, jnp.max(dmat, axis=1, keepdims=True))
        w_intra = jnp.exp(dmat - wide(m_t, l)) * scale
        w_inter = jnp.exp(inter - m_t) * scale
        qk = (_dot_nt(qh, kh) * w_intra).astype(BF16)
        num = _dot(qk, vh) + wide(w_inter, dv) * _dot(qh, ct.astype(BF16))
        den = _dot(qk, ones_l) + w_inter * _dot(qh, nmat.astype(BF16))
        inv = 1.0 / jnp.maximum(jnp.abs(den), jnp.exp(-m_t))
        msq = _dot((num * num).astype(BF16), ones_dv) * (1.0 / dv)
        norm = inv * lax.rsqrt(inv * inv * msq + EPS)

        dec = btot - bc + ic
        m_new = jnp.maximum(btot + m_st, jnp.max(dec, axis=0, keepdims=True))
        ws = jnp.exp(dec - m_new)
        wc = jnp.exp(btot + m_st - m_new)
        kwt = (kh.astype(F32) * ws).T.astype(BF16)
        ct_s[hh] = wide(wc, dv) * ct + _dot(kwt, vh)
        n_s[hh] = wc * nmat + _dot(kwt, ones_l)
        m_s[hh] = m_new

        cols = slice(hh * dv, (hh + 1) * dv)
        y_ref[0, :, cols] = (num * wide(norm, dv) * gout_ref[hh:hh + 1, :] * og[:, cols]).astype(y_ref.dtype)


def mlstm_core(q, k, v, o, gates, b_gates, g_out):
    b, s, _ = q.shape
    h, l = A_HEADS, A_CHUNK if s % A_CHUNK == 0 else s
    assert l % LANES == 0
    nc = s // l
    gates_row = gates[..., :2 * h].reshape(b, nc, l, 2 * h).transpose(0, 1, 3, 2)
    bc = jnp.zeros((1, LANES), F32).at[0, :2 * h].set(b_gates)
    br = b_gates.reshape(2 * h, 1)
    sq, sv = h * A_DQK, h * A_DV
    return pl.pallas_call(
        _mlstm_kernel, grid=(b, nc),
        in_specs=[pl.BlockSpec((1, l, sq), lambda i, c: (i, c, 0)), pl.BlockSpec((1, l, sq), lambda i, c: (i, c, 0)),
                  pl.BlockSpec((1, l, sv), lambda i, c: (i, c, 0)), pl.BlockSpec((1, l, sv), lambda i, c: (i, c, 0)),
                  pl.BlockSpec((1, l, LANES), lambda i, c: (i, c, 0)),
                  pl.BlockSpec((1, 1, 2 * h, l), lambda i, c: (i, c, 0, 0)),
                  pl.BlockSpec((1, LANES), lambda i, c: (0, 0)), pl.BlockSpec((2 * h, 1), lambda i, c: (0, 0)),
                  pl.BlockSpec((h, A_DV), lambda i, c: (0, 0))],
        out_specs=pl.BlockSpec((1, l, sv), lambda i, c: (i, c, 0)),
        out_shape=jax.ShapeDtypeStruct((b, s, sv), BF16),
        scratch_shapes=[pltpu.VMEM((h, A_DQK, A_DV), F32), pltpu.VMEM((h, A_DQK, LANES), F32),
                        pltpu.VMEM((h, 1, LANES), F32)],
        compiler_params=_params("parallel", "arbitrary"), name="mlstm_core",
    )(q, k, v, o, gates, gates_row, bc, br, g_out)


def _mlstm_epilogue(ys):
    o, q, k, v, gates = ys
    return [_sigmoid(o), q, k, v, gates]


def mixer_a(h3d, g_norm, w_in, b_gates, g_out):
    b, s, d = h3d.shape
    hh = A_HEADS
    sq, sv = hh * A_DQK, hh * A_DV
    w = jnp.pad(w_in.astype(BF16), ((0, 0), (0, LANES - 2 * hh)))
    segments = [(2 * sq + sv, sv), (0, sq), (sq, sq), (2 * sq, sv), (2 * sq + 2 * sv, LANES)]
    x2d = h3d.reshape(b * s, d)
    o, q, k, v, gates = norm_proj(x2d, g_norm, w, segments, [BF16, BF16, BF16, BF16, F32], epilogue=_mlstm_epilogue)
    y = mlstm_core(q.reshape(b, s, sq), k.reshape(b, s, sq), v.reshape(b, s, sv), o.reshape(b, s, sv),
                   gates.reshape(b, s, LANES), b_gates, g_out)
    return y.reshape(b * s, sv)


def _moba_kernel(q_ref, k_ref, v_ref, o_ref, km_s, *, topk):
    s, dh = q_ref.shape[1], q_ref.shape[2]
    nblk = s // B_BLOCK
    qn = q_ref[0]
    kn = k_ref[0]
    qb16 = qn.astype(BF16)
    kb16 = kn.astype(BF16)
    km_s[...] = jnp.zeros_like(km_s)
    for n in range(nblk):
        km_s[n:n + 1, :] = jnp.mean(kn[n * B_BLOCK:(n + 1) * B_BLOCK, :], axis=0, keepdims=True)

    nb8 = -(-nblk // 8) * 8
    blk = lax.broadcasted_iota(jnp.int32, (nb8, B_BLOCK), 0)
    tri = (lax.broadcasted_iota(jnp.int32, (B_BLOCK, B_BLOCK), 0)
           >= lax.broadcasted_iota(jnp.int32, (B_BLOCK, B_BLOCK), 1))
    for j in range(nblk):
        rows = slice(j * B_BLOCK, (j + 1) * B_BLOCK)
        qj = qb16[rows]
        select = j > topk
        if select:
            gate = lax.dot_general(km_s[:nb8, :], qn[rows], (((1,), (1,)), ((), ())), precision=HIGHEST,
                                   preferred_element_type=F32)
            gate = jnp.where(blk < j, gate, NEG_INF)
            rank = jnp.zeros(gate.shape, F32)
            for m in range(j):
                gm = gate[m:m + 1, :]
                rank = rank + jnp.where((gm > gate) | ((gm == gate) & (blk > m)), 1.0, 0.0)
            bias = jnp.where(rank < topk, 0.0, NEG_INF)
            bias = jnp.concatenate([bias, jnp.zeros((LANES - nb8, B_BLOCK), F32)], axis=0).T
        parts = []
        for n in range(j):
            sc = _dot_nt(qj, kb16[n * B_BLOCK:(n + 1) * B_BLOCK])
            parts.append(sc + bias[:, n:n + 1] if select else sc)
        parts.append(jnp.where(tri, _dot_nt(qj, kb16[rows]), NEG_INF))
        sc = jnp.concatenate(parts, axis=1) if j else parts[0]
        p = jnp.exp(sc - jnp.max(sc, axis=1, keepdims=True))
        out = _dot(p.astype(BF16), v_ref[0, :(j + 1) * B_BLOCK, :]) / jnp.sum(p, axis=1, keepdims=True)
        o_ref[0, rows, :] = out.astype(o_ref.dtype)


def moba_core(q, k, v):
    b, s, hd = q.shape
    h, dh = B_HEADS, B_DH
    assert s % B_BLOCK == 0 and s // B_BLOCK <= LANES
    topk = min(B_TOPK, s // B_BLOCK - 1)
    head_spec = pl.BlockSpec((1, s, dh), lambda i, hh: (i, 0, hh))
    return pl.pallas_call(
        functools.partial(_moba_kernel, topk=topk), grid=(b, h),
        in_specs=[head_spec, head_spec, head_spec],
        out_specs=head_spec,
        out_shape=jax.ShapeDtypeStruct((b, s, hd), BF16),
        scratch_shapes=[pltpu.VMEM((LANES, dh), F32)],
        compiler_params=_params("parallel", "parallel"), name="moba_core",
    )(q, k, v)


def _moba_epilogue(ys, pos_ref, inv_ref, gq_ref, gk_ref):
    q, k, v = ys
    dh, half = B_DH, B_ROPE_DIMS // 2
    ang = pos_ref[...] * inv_ref[...]
    first = lax.broadcasted_iota(jnp.int32, ang.shape, 1) < half
    cos = jnp.cos(ang)
    sin = jnp.where(first, -jnp.sin(ang), jnp.sin(ang))

    def prep(x, g_ref, scale):
        parts = []
        for hh in range(B_HEADS):
            xh = _rms(x[:, hh * dh:(hh + 1) * dh], g_ref[...])
            partner = jnp.where(first, pltpu.roll(xh, dh - half, 1), pltpu.roll(xh, half, 1))
            parts.append((xh * cos + partner * sin) * scale)
        return jnp.concatenate(parts, axis=1)

    return [prep(q, gq_ref, dh ** -0.5), prep(k, gk_ref, 1.0), v]


def mixer_b(h3d, positions, g_norm, w_in, g_q, g_k):
    b, s, d = h3d.shape
    hd, dh, half = B_HEADS * B_DH, B_DH, B_ROPE_DIMS // 2
    w = w_in.astype(BF16)
    x2d = h3d.reshape(b * s, d)
    inv = B_ROPE_THETA ** (-jnp.arange(half, dtype=F32) * (2.0 / B_ROPE_DIMS))
    inv_row = jnp.zeros((1, dh), F32).at[0, :half].set(inv).at[0, half:2 * half].set(inv)
    pos = positions.astype(F32).reshape(b * s, 1)
    q, k, v = norm_proj(x2d, g_norm, w, [(0, hd), (hd, hd), (2 * hd, hd)], [F32, F32, BF16],
                        epilogue=_moba_epilogue, row_aux=[pos], const_aux=[inv_row, g_q.reshape(1, dh), g_k.reshape(1, dh)])
    y = moba_core(q.reshape(b, s, hd), k.reshape(b, s, hd), v.reshape(b, s, hd))
    return y.reshape(b * s, hd)


def _retention_kernel(q_ref, k_ref, v_ref, g_ref, gout_ref, y_ref, r_s, decay_s):
    l = q_ref.shape[1]
    h, dk, dv = C_HEADS, C_DK, C_DV
    log_g = [math.log1p(-(2.0 ** (-5.0 - hh))) for hh in range(h)]

    @pl.when(pl.program_id(1) == 0)
    def _():
        r_s[...] = jnp.zeros_like(r_s)
        diff = (lax.broadcasted_iota(jnp.int32, (l, l), 0) - lax.broadcasted_iota(jnp.int32, (l, l), 1)).astype(F32)
        for hh in range(h):
            decay_s[hh] = jnp.where(diff >= 0, jnp.exp(jnp.maximum(diff, 0.0) * log_g[hh]), 0.0)

    t_col = lax.broadcasted_iota(jnp.int32, (l, 1), 0).astype(F32)
    q = q_ref[0]
    k = k_ref[0]
    v = v_ref[0]
    g = g_ref[0].astype(F32)
    for hh in range(h):
        xi = jnp.exp((t_col + 1.0) * log_g[hh])
        zeta = jnp.exp((l - 1.0 - t_col) * log_g[hh])
        qh = q[:, hh * dk:(hh + 1) * dk]
        kh = k[:, hh * dk:(hh + 1) * dk]
        vh = v[:, hh * dv:(hh + 1) * dv]
        r_st = r_s[hh]
        inner = _dot((_dot_nt(qh, kh) * decay_s[hh]).astype(BF16), vh)
        cross = _dot(qh, r_st.astype(BF16)) * xi
        r_s[hh] = math.exp(l * log_g[hh]) * r_st + _dot((kh.astype(F32) * zeta).T.astype(BF16), vh)
        ys = inner + cross
        mu = jnp.mean(ys, axis=1, keepdims=True)
        yc = ys - mu
        yn = yc * lax.rsqrt(jnp.mean(yc * yc, axis=1, keepdims=True) + EPS) * gout_ref[hh:hh + 1, :]
        y_ref[0, :, hh * dv:(hh + 1) * dv] = (yn * g[:, hh * dv:(hh + 1) * dv]).astype(y_ref.dtype)


def retention_core(q, k, v, g, g_out):
    b, s, _ = q.shape
    h, l = C_HEADS, C_CHUNK if s % C_CHUNK == 0 else s
    sk, sv = h * C_DK, h * C_DV
    return pl.pallas_call(
        _retention_kernel, grid=(b, s // l),
        in_specs=[pl.BlockSpec((1, l, sk), lambda i, c: (i, c, 0)), pl.BlockSpec((1, l, sk), lambda i, c: (i, c, 0)),
                  pl.BlockSpec((1, l, sv), lambda i, c: (i, c, 0)), pl.BlockSpec((1, l, sv), lambda i, c: (i, c, 0)),
                  pl.BlockSpec((h, C_DV), lambda i, c: (0, 0))],
        out_specs=pl.BlockSpec((1, l, sv), lambda i, c: (i, c, 0)),
        out_shape=jax.ShapeDtypeStruct((b, s, sv), BF16),
        scratch_shapes=[pltpu.VMEM((h, C_DK, C_DV), F32), pltpu.VMEM((h, l, l), F32)],
        compiler_params=_params("parallel", "arbitrary"), name="retention_core",
    )(q, k, v, g, g_out)


def _retention_epilogue(ys, pos_ref, inv_ref):
    q, k, g, v = ys
    half = C_DK // 2
    ang = pos_ref[...] * inv_ref[...]
    cos, sin = jnp.cos(ang), jnp.sin(ang)

    def rot(x, scale):
        parts = []
        for hh in range(C_HEADS):
            x1, x2 = x[:, hh * C_DK:hh * C_DK + half], x[:, hh * C_DK + half:(hh + 1) * C_DK]
            parts += [(x1 * cos - x2 * sin) * scale, (x2 * cos + x1 * sin) * scale]
        return jnp.concatenate(parts, axis=1)

    return [rot(q, 1.0), rot(k, C_DK ** -0.5), _silu(g), v]


def mixer_c(h3d, positions, g_norm, w_in, g_out):
    b, s, d = h3d.shape
    sk, sv = C_HEADS * C_DK, C_HEADS * C_DV
    half = C_DK // 2
    w = w_in.astype(BF16)
    x2d = h3d.reshape(b * s, d)
    segments = [(0, sk), (sk, sk), (2 * sk + sv, sv), (2 * sk, sv)]
    inv = (C_ROPE_THETA ** (-jnp.arange(half, dtype=F32) * (2.0 / C_DK))).reshape(1, half)
    pos = positions.astype(F32).reshape(b * s, 1)
    q, k, g, v = norm_proj(x2d, g_norm, w, segments, [BF16, BF16, BF16, BF16], epilogue=_retention_epilogue,
                           row_aux=[pos], const_aux=[inv])
    y = retention_core(q.reshape(b, s, sk), k.reshape(b, s, sk), v.reshape(b, s, sv), g.reshape(b, s, sv), g_out)
    return y.reshape(b * s, sv)


D_SEGMENTS = 8


def _rglru_kernel(xb_ref, gb_ref, cw_ref, cb_ref, wg_ref, bg_ref, lru_ref, y_ref, a_s, b_s):
    rows, c = xb_ref.shape[1], xb_ref.shape[2]
    nseg, bw = D_SEGMENTS, D_BW
    seg = rows // nseg
    x = xb_ref[0].reshape(nseg, seg, c).transpose(1, 0, 2).reshape(rows, c)
    ntail = (D_CONV - 1) * nseg
    tail = x[rows - ntail:, :]
    seg_id = lax.broadcasted_iota(jnp.int32, (ntail, c), 0) & (nseg - 1)
    head = jnp.where(seg_id >= 1, pltpu.roll(tail, 1, 0), 0.0)
    xe = jnp.concatenate([head, x], axis=0)
    xc = x * cw_ref[D_CONV - 1:D_CONV, :] + cb_ref[...]
    for d in range(1, D_CONV):
        xc = xc + xe[ntail - d * nseg:ntail - d * nseg + rows, :] * cw_ref[D_CONV - 1 - d:D_CONV - d, :]
    sp = _softplus(-lru_ref[...])
    for n in range(c // bw):
        cols = slice(n * bw, (n + 1) * bw)
        xcn = xc[:, cols]
        gates = _dot(xcn.astype(BF16), wg_ref[n]) + bg_ref[n]
        r = _sigmoid(gates[:, :bw])
        i = _sigmoid(gates[:, bw:])
        log_a = (-LRU_C * sp[:, cols]) * r
        a = jnp.exp(log_a)
        a_s[:, cols] = a
        var = jnp.tanh(-log_a) * (1.0 + a * a)
        b_s[:, cols] = jnp.where(var > 0.0, var * lax.rsqrt(var), 0.0) * (i * xcn)

    def slab(m):
        return pl.ds(pl.multiple_of(m * nseg, nseg), nseg)

    def totals(m, carry):
        h, acc = carry
        am = a_s[slab(m), :]
        return am * h + b_s[slab(m), :], am * acc

    h_loc, a_tot = lax.fori_loop(0, seg, totals, (jnp.zeros((nseg, c), F32), jnp.ones((nseg, c), F32)), unroll=8)
    carry_in = [jnp.zeros((1, c), F32)]
    for r in range(1, nseg):
        carry_in.append(h_loc[r - 1:r, :] + a_tot[r - 1:r, :] * carry_in[-1])

    def scan(m, h):
        h = a_s[slab(m), :] * h + b_s[slab(m), :]
        b_s[slab(m), :] = h
        return h

    lax.fori_loop(0, seg, scan, jnp.concatenate(carry_in, axis=0), unroll=8)
    h = b_s[...].reshape(seg, nseg, c).transpose(1, 0, 2).reshape(rows, c)
    y_ref[0] = (h * gb_ref[0].astype(F32)).astype(y_ref.dtype)


def rglru_core(gate_br, xb, conv_w, conv_b, w_gates, b_gates, lru):
    b, s, dr = xb.shape
    c = 2 * D_BW
    nb = c // D_BW
    chan = lambda i, n: (i, 0, n)
    return pl.pallas_call(
        _rglru_kernel, grid=(b, dr // c),
        in_specs=[pl.BlockSpec((1, s, c), chan), pl.BlockSpec((1, s, c), chan),
                  pl.BlockSpec((D_CONV, c), lambda i, n: (0, n)), pl.BlockSpec((1, c), lambda i, n: (0, n)),
                  pl.BlockSpec((nb, D_BW, 2 * D_BW), lambda i, n: (n, 0, 0)),
                  pl.BlockSpec((nb, 1, 2 * D_BW), lambda i, n: (n, 0, 0)),
                  pl.BlockSpec((1, c), lambda i, n: (0, n))],
        out_specs=pl.BlockSpec((1, s, c), chan),
        out_shape=jax.ShapeDtypeStruct((b, s, dr), BF16),
        scratch_shapes=[pltpu.VMEM((s, c), F32), pltpu.VMEM((s, c), F32)],
        compiler_params=_params("parallel", "parallel"), name="rglru_core",
    )(xb, gate_br, conv_w, conv_b.reshape(1, dr), w_gates.astype(BF16), b_gates.reshape(D_BLOCKS, 1, 2 * D_BW),
      lru.reshape(1, dr))


def _rglru_epilogue(ys):
    gate_br, xb = ys
    return [jax.nn.gelu(gate_br), xb]


def mixer_d(h3d, g_norm, w_in, conv_w, conv_b, w_gates, b_gates, lru):
    b, s, d = h3d.shape
    dr = D_BLOCKS * D_BW
    assert s % (8 * D_SEGMENTS) == 0
    w = w_in.astype(BF16)
    x2d = h3d.reshape(b * s, d)
    gate_br, xb = norm_proj(x2d, g_norm, w, [(0, dr), (dr, dr)], [BF16, F32], epilogue=_rglru_epilogue)
    y = rglru_core(gate_br.reshape(b, s, dr), xb.reshape(b, s, dr), conv_w, conv_b, w_gates, b_gates, lru)
    return y.reshape(b * s, dr)


def ffn_layer(y2d, w_mix, h3d, g_norm, w_in, w_out):
    b, s, d = h3d.shape
    return mix_out_ffn(y2d, w_mix.astype(BF16), h3d.reshape(b * s, d), g_norm, w_in.astype(BF16),
                       w_out.astype(BF16)).reshape(b, s, d)


def kernel(x, positions, norm_mix, norm_ffn, ffn_w_in, ffn_w_out, a_w_in, a_b_gates, a_g_out, a_w_out, b_w_in, b_g_q, b_g_k, b_w_out, c_w_in, c_g_out, c_w_out, d_w_in, d_conv_w, d_conv_b, d_w_gates, d_b_gates, d_lru, d_w_out):
    h = x
    depth = norm_mix.shape[0]
    for layer in range(depth):
        mixer, r = layer % 4, layer // 4
        if mixer == 0:
            y, w_mix = mixer_a(h, norm_mix[layer], a_w_in[r], a_b_gates[r], a_g_out[r]), a_w_out[r]
        elif mixer == 1:
            y, w_mix = mixer_b(h, positions, norm_mix[layer], b_w_in[r], b_g_q[r], b_g_k[r]), b_w_out[r]
        elif mixer == 2:
            y, w_mix = mixer_c(h, positions, norm_mix[layer], c_w_in[r], c_g_out[r]), c_w_out[r]
        else:
            y, w_mix = mixer_d(h, norm_mix[layer], d_w_in[r], d_conv_w[r], d_conv_b[r], d_w_gates[r], d_b_gates[r],
                               d_lru[r]), d_w_out[r]
        h = ffn_layer(y, w_mix, h, norm_ffn[layer], ffn_w_in[layer], ffn_w_out[layer])
    return h
```

```python
import functools
import math

import jax
import jax.numpy as jnp
from jax import lax
from jax.experimental import pallas as pl
from jax.experimental.pallas import tpu as pltpu

F32 = jnp.float32
BF16 = jnp.bfloat16
EPS = 1e-6
NEG_INF = float("-inf")

LANES = 128
MXU_DIM = 256
VMEM_LIMIT_BYTES = 56 * 1024 * 1024

A_HEADS, A_DQK, A_DV = 4, 128, 256
A_CHUNK = 256
B_HEADS, B_DH, B_BLOCK, B_TOPK, B_QBLOCK = 8, 128, 256, 3, 128
B_ROPE_THETA, B_ROPE_DIMS = 500000.0, 32
C_HEADS, C_DK, C_DV, C_CHUNK = 4, 256, 512, 256
C_ROPE_THETA = 10000.0
D_BLOCKS, D_BW, D_CONV, LRU_C = 4, 256, 4, 8.0


def _params(*semantics):
    return pltpu.CompilerParams(dimension_semantics=semantics, vmem_limit_bytes=VMEM_LIMIT_BYTES)


def _row_tile(t, want=512):
    return want if t % want == 0 else t


def _sigmoid(x):
    return 0.5 * jnp.tanh(0.5 * x) + 0.5


def _silu(x):
    return x * _sigmoid(x)


def _softplus(x):
    return jnp.maximum(x, 0.0) + jnp.log1p(jnp.exp(-jnp.abs(x)))


def _rms(x, g):
    return x * lax.rsqrt(jnp.mean(x * x, axis=-1, keepdims=True) + EPS) * g


def _dot(a, b):
    return jnp.dot(a, b, preferred_element_type=F32)


def _dot_nt(a, b):
    return lax.dot_general(a, b, (((1,), (1,)), ((), ())), preferred_element_type=F32)


def _norm_proj_kernel(x_ref, g_ref, w_ref, *refs, segments, epilogue):
    n = len(segments)
    aux_refs, o_refs = refs[:len(refs) - n], refs[len(refs) - n:]
    u = _rms(x_ref[...], g_ref[...]).astype(BF16)
    ys = [_dot(u, w_ref[:, start:start + width]) for start, width in segments]
    if epilogue is not None:
        ys = epilogue(ys, *aux_refs)
    for y, o_ref in zip(ys, o_refs):
        o_ref[...] = y.astype(o_ref.dtype)


def _resident(shape):
    return pl.BlockSpec(shape, lambda i: (0,) * len(shape), pipeline_mode=pl.Buffered(1))


def norm_proj(x2d, g, w, segments, out_dtypes, epilogue=None, row_aux=(), const_aux=()):
    t, d = x2d.shape
    tm = _row_tile(t)
    in_specs = [pl.BlockSpec((tm, d), lambda i: (i, 0)), _resident((1, d)), _resident(w.shape)]
    in_specs += [pl.BlockSpec((tm, a.shape[1]), lambda i: (i, 0)) for a in row_aux]
    in_specs += [_resident(a.shape) for a in const_aux]
    out_specs = [pl.BlockSpec((tm, width), lambda i: (i, 0)) for _, width in segments]
    out_shape = [jax.ShapeDtypeStruct((t, width), dt) for (_, width), dt in zip(segments, out_dtypes)]
    return pl.pallas_call(
        functools.partial(_norm_proj_kernel, segments=tuple(segments), epilogue=epilogue), grid=(t // tm,),
        in_specs=in_specs, out_specs=out_specs, out_shape=out_shape,
        compiler_params=_params("parallel"), name="norm_proj",
    )(x2d, g.reshape(1, d), w, *row_aux, *const_aux)


def _ffn_kernel(y_ref, wm_ref, x_ref, g_ref, wi_ref, wo_ref, o_ref, *, n_chunks):
    x = x_ref[...] + _dot(y_ref[...], wm_ref[...])
    u = _rms(x, g_ref[...]).astype(BF16)
    dff = wo_ref.shape[0]
    fc = dff // n_chunks
    acc = x
    for c in range(n_chunks):
        gt = _dot(u, wi_ref[:, c * fc:(c + 1) * fc])
        up = _dot(u, wi_ref[:, dff + c * fc:dff + (c + 1) * fc])
        act = (_silu(gt) * up).astype(BF16)
        acc = acc + _dot(act, wo_ref[c * fc:(c + 1) * fc, :])
    o_ref[...] = acc


def mix_out_ffn(y2d, w_mix, x2d, g, w_in, w_out, layer):
    t, d = x2d.shape
    k, dff = w_mix.shape[0], w_out.shape[1]
    tm = _row_tile(t)
    n_chunks = dff // MXU_DIM if dff % MXU_DIM == 0 else 1

    def of_layer(rows, cols):
        return pl.BlockSpec((None, rows, cols), lambda i: (layer, 0, 0), pipeline_mode=pl.Buffered(1))

    return pl.pallas_call(
        functools.partial(_ffn_kernel, n_chunks=n_chunks), grid=(t // tm,),
        in_specs=[pl.BlockSpec((tm, k), lambda i: (i, 0)), _resident((k, d)),
                  pl.BlockSpec((tm, d), lambda i: (i, 0)), _resident((1, d)),
                  of_layer(d, 2 * dff), of_layer(dff, d)],
        out_specs=pl.BlockSpec((tm, d), lambda i: (i, 0)),
        out_shape=jax.ShapeDtypeStruct((t, d), F32),
        compiler_params=_params("parallel"), name="mix_out_ffn",
    )(y2d, w_mix, x2d, g.reshape(1, d), w_in, w_out)


def _mlstm_kernel(q_ref, k_ref, v_ref, o_ref, gc_ref, bc_ref, gout_ref, y_ref, ct_s, n_s, m_s):
    l = q_ref.shape[1]
    h, dqk, dv = A_HEADS, A_DQK, A_DV
    scale = dqk ** -0.5

    @pl.when(pl.program_id(1) == 0)
    def _():
        ct_s[...] = jnp.zeros_like(ct_s)
        n_s[...] = jnp.zeros_like(n_s)
        m_s[...] = jnp.zeros_like(m_s)

    gc = gc_ref[0] + bc_ref[...]
    lf_c = jnp.minimum(gc, 0.0) - jnp.log1p(jnp.exp(-jnp.abs(gc)))
    causal = lax.broadcasted_iota(jnp.int32, (l, l), 0) >= lax.broadcasted_iota(jnp.int32, (l, l), 1)
    tril = causal.astype(BF16)
    lf_hi = lf_c.astype(BF16)
    lf_lo = (lf_c - lf_hi.astype(F32)).astype(BF16)
    bcum_c = _dot(tril, lf_hi) + _dot(tril, lf_lo)
    gr = gc.T[:2 * h, :]
    bcum_r = bcum_c.T[:2 * h, :]

    def wide(x, width):
        return jnp.concatenate([x] * (width // LANES), axis=1)

    ones_l = jnp.ones((l, LANES), BF16)
    ones_dv = jnp.ones((dv, LANES), BF16)
    q = q_ref[0]
    k = k_ref[0]
    v = v_ref[0]
    og = o_ref[0].astype(F32)
    for hh in range(h):
        bc = jnp.broadcast_to(bcum_c[:, h + hh:h + hh + 1], (l, LANES))
        ic = jnp.broadcast_to(gc[:, hh:hh + 1], (l, LANES))
        br = bcum_r[h + hh:h + hh + 1, :]
        ir = gr[hh:hh + 1, :]
        btot = bc[l - 1:l, :]
        m_st = m_s[hh]
        qh = q[:, hh * dqk:(hh + 1) * dqk]
        kh = k[:, hh * dqk:(hh + 1) * dqk]
        vh = v[:, hh * dv:(hh + 1) * dv]
        ct = ct_s[hh]
        nmat = n_s[hh]

        dmat = jnp.where(causal, wide(bc, l) - br + ir, NEG_INF)
        inter = bc + m_st
        m_t = jnp.maximum(inter, jnp.max(dmat, axis=1, keepdims=True))
        w_intra = jnp.exp(dmat - wide(m_t, l)) * scale
        w_inter = jnp.exp(inter - m_t) * scale
        qk = (_dot_nt(qh, kh) * w_intra).astype(BF16)
        num = _dot(qk, vh) + wide(w_inter, dv) * _dot(qh, ct.astype(BF16))
        den = _dot(qk, ones_l) + w_inter * _dot(qh, nmat.astype(BF16))
        inv = 1.0 / jnp.maximum(jnp.abs(den), jnp.exp(-m_t))
        msq = _dot((num * num).astype(BF16), ones_dv) * (1.0 / dv)
        norm = inv * lax.rsqrt(inv * inv * msq + EPS)

        dec = btot - bc + ic
        m_new = jnp.maximum(btot + m_st, jnp.max(dec, axis=0, keepdims=True))
        ws = jnp.exp(dec - m_new)
        wc = jnp.exp(btot + m_st - m_new)
        kwt = (kh.astype(F32) * ws).T.astype(BF16)
        ct_s[hh] = wide(wc, dv) * ct + _dot(kwt, vh)
        n_s[hh] = wc * nmat + _dot(kwt, ones_l)
        m_s[hh] = m_new

        cols = slice(hh * dv, (hh + 1) * dv)
        y_ref[0, :, cols] = (num * wide(norm, dv) * gout_ref[hh:hh + 1, :] * og[:, cols]).astype(y_ref.dtype)


def mlstm_core(q, k, v, o, gates, b_gates, g_out):
    b, s, _ = q.shape
    h, l = A_HEADS, A_CHUNK if s % A_CHUNK == 0 else s
    assert l % LANES == 0
    nc = s // l
    bc = jnp.zeros((1, LANES), F32).at[0, :2 * h].set(b_gates)
    sq, sv = h * A_DQK, h * A_DV
    return pl.pallas_call(
        _mlstm_kernel, grid=(b, nc),
        in_specs=[pl.BlockSpec((1, l, sq), lambda i, c: (i, c, 0)), pl.BlockSpec((1, l, sq), lambda i, c: (i, c, 0)),
                  pl.BlockSpec((1, l, sv), lambda i, c: (i, c, 0)), pl.BlockSpec((1, l, sv), lambda i, c: (i, c, 0)),
                  pl.BlockSpec((1, l, LANES), lambda i, c: (i, c, 0)),
                  pl.BlockSpec((1, LANES), lambda i, c: (0, 0)), pl.BlockSpec((h, A_DV), lambda i, c: (0, 0))],
        out_specs=pl.BlockSpec((1, l, sv), lambda i, c: (i, c, 0)),
        out_shape=jax.ShapeDtypeStruct((b, s, sv), BF16),
        scratch_shapes=[pltpu.VMEM((h, A_DQK, A_DV), F32), pltpu.VMEM((h, A_DQK, LANES), F32),
                        pltpu.VMEM((h, 1, LANES), F32)],
        compiler_params=_params("parallel", "arbitrary"), name="mlstm_core",
    )(q, k, v, o, gates, bc, g_out)


def _mlstm_epilogue(ys):
    o, q, k, v, gates = ys
    return [_sigmoid(o), q, k, v, gates]


def mixer_a(h3d, g_norm, w_in, b_gates, g_out):
    b, s, d = h3d.shape
    hh = A_HEADS
    sq, sv = hh * A_DQK, hh * A_DV
    w = jnp.pad(w_in.astype(BF16), ((0, 0), (0, LANES - 2 * hh)))
    segments = [(2 * sq + sv, sv), (0, sq), (sq, sq), (2 * sq, sv), (2 * sq + 2 * sv, LANES)]
    x2d = h3d.reshape(b * s, d)
    o, q, k, v, gates = norm_proj(x2d, g_norm, w, segments, [BF16, BF16, BF16, BF16, F32], epilogue=_mlstm_epilogue)
    y = mlstm_core(q.reshape(b, s, sq), k.reshape(b, s, sq), v.reshape(b, s, sv), o.reshape(b, s, sv),
                   gates.reshape(b, s, LANES), b_gates, g_out)
    return y.reshape(b * s, sv)


def _moba_kernel(q_ref, k_ref, v_ref, o_ref, km_s, *, topk):
    s, dh = q_ref.shape[1], q_ref.shape[2]
    nblk = s // B_BLOCK
    qn = q_ref[0]
    kn = k_ref[0]
    qb16 = qn.astype(BF16)
    kb16 = kn.astype(BF16)
    km_s[...] = jnp.zeros_like(km_s)
    for n in range(nblk):
        km_s[n:n + 1, :] = jnp.mean(kn[n * B_BLOCK:(n + 1) * B_BLOCK, :], axis=0, keepdims=True)

    nb8 = -(-nblk // 8) * 8
    km = km_s[:nb8, :]
    km_hi = km.astype(BF16)
    km_lo = (km - km_hi.astype(F32)).astype(BF16)
    blk = lax.broadcasted_iota(jnp.int32, (nb8, B_BLOCK), 0)
    tri = (lax.broadcasted_iota(jnp.int32, (B_BLOCK, B_BLOCK), 0)
           >= lax.broadcasted_iota(jnp.int32, (B_BLOCK, B_BLOCK), 1))
    for j in range(nblk):
        rows = slice(j * B_BLOCK, (j + 1) * B_BLOCK)
        qj = qb16[rows]
        select = j > topk
        if select:
            q_lo = (qn[rows] - qj.astype(F32)).astype(BF16)
            gate = _dot_nt(km_hi, qj) + _dot_nt(km_hi, q_lo) + _dot_nt(km_lo, qj)
            gate = jnp.where(blk < j, gate, NEG_INF)
            rank = jnp.zeros(gate.shape, F32)
            for m in range(j):
                gm = gate[m:m + 1, :]
                rank = rank + jnp.where((gm > gate) | ((gm == gate) & (blk > m)), 1.0, 0.0)
            bias = jnp.where(rank < topk, 0.0, NEG_INF)
            bias = jnp.concatenate([bias, jnp.zeros((LANES - nb8, B_BLOCK), F32)], axis=0).T
        parts = []
        for n in range(j):
            sc = _dot_nt(qj, kb16[n * B_BLOCK:(n + 1) * B_BLOCK])
            parts.append(sc + bias[:, n:n + 1] if select else sc)
        parts.append(jnp.where(tri, _dot_nt(qj, kb16[rows]), NEG_INF))
        sc = jnp.concatenate(parts, axis=1) if j else parts[0]
        p = jnp.exp2(sc - jnp.max(sc, axis=1, keepdims=True))
        out = _dot(p.astype(BF16), v_ref[0, :(j + 1) * B_BLOCK, :]) * (1.0 / jnp.sum(p, axis=1, keepdims=True))
        o_ref[0, rows, :] = out.astype(o_ref.dtype)


def moba_core(q, k, v):
    b, s, hd = q.shape
    h, dh = B_HEADS, B_DH
    assert s % B_BLOCK == 0 and s // B_BLOCK <= LANES
    topk = min(B_TOPK, s // B_BLOCK - 1)
    head_spec = pl.BlockSpec((1, s, dh), lambda i, hh: (i, 0, hh))
    return pl.pallas_call(
        functools.partial(_moba_kernel, topk=topk), grid=(b, h),
        in_specs=[head_spec, head_spec, head_spec],
        out_specs=head_spec,
        out_shape=jax.ShapeDtypeStruct((b, s, hd), BF16),
        scratch_shapes=[pltpu.VMEM((LANES, dh), F32)],
        compiler_params=_params("parallel", "parallel"), name="moba_core",
    )(q, k, v)


def _moba_epilogue(ys, pos_ref, inv_ref, gq_ref, gk_ref):
    q, k, v = ys
    dh, half = B_DH, B_ROPE_DIMS // 2
    ang = pos_ref[...] * inv_ref[...]
    first = lax.broadcasted_iota(jnp.int32, ang.shape, 1) < half
    cos = jnp.cos(ang)
    sin = jnp.where(first, -jnp.sin(ang), jnp.sin(ang))

    def prep(x, g_ref, scale):
        parts = []
        for hh in range(B_HEADS):
            xh = _rms(x[:, hh * dh:(hh + 1) * dh], g_ref[...])
            partner = jnp.where(first, pltpu.roll(xh, dh - half, 1), pltpu.roll(xh, half, 1))
            parts.append((xh * cos + partner * sin) * scale)
        return jnp.concatenate(parts, axis=1)

    return [prep(q, gq_ref, dh ** -0.5 * math.log2(math.e)), prep(k, gk_ref, 1.0), v]


def mixer_b(h3d, positions, g_norm, w_in, g_q, g_k):
    b, s, d = h3d.shape
    hd, dh, half = B_HEADS * B_DH, B_DH, B_ROPE_DIMS // 2
    w = w_in.astype(BF16)
    x2d = h3d.reshape(b * s, d)
    inv = B_ROPE_THETA ** (-jnp.arange(half, dtype=F32) * (2.0 / B_ROPE_DIMS))
    inv_row = jnp.zeros((1, dh), F32).at[0, :half].set(inv).at[0, half:2 * half].set(inv)
    pos = positions.astype(F32).reshape(b * s, 1)
    q, k, v = norm_proj(x2d, g_norm, w, [(0, hd), (hd, hd), (2 * hd, hd)], [F32, F32, BF16],
                        epilogue=_moba_epilogue, row_aux=[pos], const_aux=[inv_row, g_q.reshape(1, dh), g_k.reshape(1, dh)])
    y = moba_core(q.reshape(b, s, hd), k.reshape(b, s, hd), v.reshape(b, s, hd))
    return y.reshape(b * s, hd)


def _retention_kernel(q_ref, k_ref, v_ref, g_ref, gout_ref, y_ref, r_s, decay_s):
    l = q_ref.shape[1]
    h, dk, dv = C_HEADS, C_DK, C_DV
    log_g = [math.log1p(-(2.0 ** (-5.0 - hh))) for hh in range(h)]

    @pl.when(pl.program_id(1) == 0)
    def _():
        r_s[...] = jnp.zeros_like(r_s)
        diff = (lax.broadcasted_iota(jnp.int32, (l, l), 0) - lax.broadcasted_iota(jnp.int32, (l, l), 1)).astype(F32)
        for hh in range(h):
            decay_s[hh] = jnp.where(diff >= 0, jnp.exp(jnp.maximum(diff, 0.0) * log_g[hh]), 0.0)

    t_col = lax.broadcasted_iota(jnp.int32, (l, 1), 0).astype(F32)
    q = q_ref[0]
    k = k_ref[0]
    v = v_ref[0]
    g = g_ref[0].astype(F32)
    for hh in range(h):
        xi = jnp.exp((t_col + 1.0) * log_g[hh])
        zeta = jnp.exp((l - 1.0 - t_col) * log_g[hh])
        qh = q[:, hh * dk:(hh + 1) * dk]
        kh = k[:, hh * dk:(hh + 1) * dk]
        vh = v[:, hh * dv:(hh + 1) * dv]
        r_st = r_s[hh]
        inner = _dot((_dot_nt(qh, kh) * decay_s[hh]).astype(BF16), vh)
        cross = _dot(qh, r_st.astype(BF16)) * xi
        r_s[hh] = math.exp(l * log_g[hh]) * r_st + _dot((kh.astype(F32) * zeta).T.astype(BF16), vh)
        ys = inner + cross
        mu = jnp.mean(ys, axis=1, keepdims=True)
        yc = ys - mu
        yn = yc * lax.rsqrt(jnp.mean(yc * yc, axis=1, keepdims=True) + EPS) * gout_ref[hh:hh + 1, :]
        y_ref[0, :, hh * dv:(hh + 1) * dv] = (yn * g[:, hh * dv:(hh + 1) * dv]).astype(y_ref.dtype)


def retention_core(q, k, v, g, g_out):
    b, s, _ = q.shape
    h, l = C_HEADS, C_CHUNK if s % C_CHUNK == 0 else s
    sk, sv = h * C_DK, h * C_DV
    return pl.pallas_call(
        _retention_kernel, grid=(b, s // l),
        in_specs=[pl.BlockSpec((1, l, sk), lambda i, c: (i, c, 0)), pl.BlockSpec((1, l, sk), lambda i, c: (i, c, 0)),
                  pl.BlockSpec((1, l, sv), lambda i, c: (i, c, 0)), pl.BlockSpec((1, l, sv), lambda i, c: (i, c, 0)),
                  pl.BlockSpec((h, C_DV), lambda i, c: (0, 0))],
        out_specs=pl.BlockSpec((1, l, sv), lambda i, c: (i, c, 0)),
        out_shape=jax.ShapeDtypeStruct((b, s, sv), BF16),
        scratch_shapes=[pltpu.VMEM((h, C_DK, C_DV), F32), pltpu.VMEM((h, l, l), F32)],
        compiler_params=_params("parallel", "arbitrary"), name="retention_core",
    )(q, k, v, g, g_out)


def _retention_epilogue(ys, pos_ref, inv_ref):
    q, k, g, v = ys
    half = C_DK // 2
    ang = pos_ref[...] * inv_ref[...]
    cos, sin = jnp.cos(ang), jnp.sin(ang)

    def rot(x, scale):
        parts = []
        for hh in range(C_HEADS):
            x1, x2 = x[:, hh * C_DK:hh * C_DK + half], x[:, hh * C_DK + half:(hh + 1) * C_DK]
            parts += [(x1 * cos - x2 * sin) * scale, (x2 * cos + x1 * sin) * scale]
        return jnp.concatenate(parts, axis=1)

    return [rot(q, 1.0), rot(k, C_DK ** -0.5), _silu(g), v]


def mixer_c(h3d, positions, g_norm, w_in, g_out):
    b, s, d = h3d.shape
    sk, sv = C_HEADS * C_DK, C_HEADS * C_DV
    half = C_DK // 2
    w = w_in.astype(BF16)
    x2d = h3d.reshape(b * s, d)
    segments = [(0, sk), (sk, sk), (2 * sk + sv, sv), (2 * sk, sv)]
    inv = (C_ROPE_THETA ** (-jnp.arange(half, dtype=F32) * (2.0 / C_DK))).reshape(1, half)
    pos = positions.astype(F32).reshape(b * s, 1)
    q, k, g, v = norm_proj(x2d, g_norm, w, segments, [BF16, BF16, BF16, BF16], epilogue=_retention_epilogue,
                           row_aux=[pos], const_aux=[inv])
    y = retention_core(q.reshape(b, s, sk), k.reshape(b, s, sk), v.reshape(b, s, sv), g.reshape(b, s, sv), g_out)
    return y.reshape(b * s, sv)


D_SEGMENTS = 8


def _rglru_kernel(xb_ref, gb_ref, cw_ref, cb_ref, wg_ref, bg_ref, lru_ref, y_ref, a_s, b_s):
    rows, c = xb_ref.shape[1], xb_ref.shape[2]
    nseg, bw = D_SEGMENTS, D_BW
    seg = rows // nseg
    x = xb_ref[0].reshape(nseg, seg, c).transpose(1, 0, 2).reshape(rows, c)
    ntail = (D_CONV - 1) * nseg
    tail = x[rows - ntail:, :]
    seg_id = lax.broadcasted_iota(jnp.int32, (ntail, c), 0) & (nseg - 1)
    head = jnp.where(seg_id >= 1, pltpu.roll(tail, 1, 0), 0.0)
    xe = jnp.concatenate([head, x], axis=0)
    xc = x * cw_ref[D_CONV - 1:D_CONV, :] + cb_ref[...]
    for d in range(1, D_CONV):
        xc = xc + xe[ntail - d * nseg:ntail - d * nseg + rows, :] * cw_ref[D_CONV - 1 - d:D_CONV - d, :]
    sp = _softplus(-lru_ref[...])
    for n in range(c // bw):
        cols = slice(n * bw, (n + 1) * bw)
        xcn = xc[:, cols]
        gates = _dot(xcn.astype(BF16), wg_ref[n]) + bg_ref[n]
        r = _sigmoid(gates[:, :bw])
        i = _sigmoid(gates[:, bw:])
        log_a = (-LRU_C * sp[:, cols]) * r
        a = jnp.exp(log_a)
        a_s[:, cols] = a
        var = jnp.tanh(-log_a) * (1.0 + a * a)
        b_s[:, cols] = jnp.where(var > 0.0, var * lax.rsqrt(var), 0.0) * (i * xcn)

    def slab(m):
        return pl.ds(pl.multiple_of(m * nseg, nseg), nseg)

    def totals(m, carry):
        h, acc = carry
        am = a_s[slab(m), :]
        return am * h + b_s[slab(m), :], am * acc

    h_loc, a_tot = lax.fori_loop(0, seg, totals, (jnp.zeros((nseg, c), F32), jnp.ones((nseg, c), F32)), unroll=8)
    carry_in = [jnp.zeros((1, c), F32)]
    for r in range(1, nseg):
        carry_in.append(h_loc[r - 1:r, :] + a_tot[r - 1:r, :] * carry_in[-1])

    def scan(m, h):
        h = a_s[slab(m), :] * h + b_s[slab(m), :]
        b_s[slab(m), :] = h
        return h

    lax.fori_loop(0, seg, scan, jnp.concatenate(carry_in, axis=0), unroll=8)
    h = b_s[...].reshape(seg, nseg, c).transpose(1, 0, 2).reshape(rows, c)
    y_ref[0] = (h * gb_ref[0].astype(F32)).astype(y_ref.dtype)


def rglru_core(gate_br, xb, conv_w, conv_b, w_gates, b_gates, lru):
    b, s, dr = xb.shape
    c = 2 * D_BW
    nb = c // D_BW
    chan = lambda i, n: (i, 0, n)
    return pl.pallas_call(
        _rglru_kernel, grid=(b, dr // c),
        in_specs=[pl.BlockSpec((1, s, c), chan), pl.BlockSpec((1, s, c), chan),
                  pl.BlockSpec((D_CONV, c), lambda i, n: (0, n)), pl.BlockSpec((1, c), lambda i, n: (0, n)),
                  pl.BlockSpec((nb, D_BW, 2 * D_BW), lambda i, n: (n, 0, 0)),
                  pl.BlockSpec((nb, 1, 2 * D_BW), lambda i, n: (n, 0, 0)),
                  pl.BlockSpec((1, c), lambda i, n: (0, n))],
        out_specs=pl.BlockSpec((1, s, c), chan),
        out_shape=jax.ShapeDtypeStruct((b, s, dr), BF16),
        scratch_shapes=[pltpu.VMEM((s, c), F32), pltpu.VMEM((s, c), F32)],
        compiler_params=_params("parallel", "parallel"), name="rglru_core",
    )(xb, gate_br, conv_w, conv_b.reshape(1, dr), w_gates.astype(BF16), b_gates.reshape(D_BLOCKS, 1, 2 * D_BW),
      lru.reshape(1, dr))


def _rglru_epilogue(ys):
    gate_br, xb = ys
    return [jax.nn.gelu(gate_br), xb]


def mixer_d(h3d, g_norm, w_in, conv_w, conv_b, w_gates, b_gates, lru):
    b, s, d = h3d.shape
    dr = D_BLOCKS * D_BW
    assert s % (8 * D_SEGMENTS) == 0
    w = w_in.astype(BF16)
    x2d = h3d.reshape(b * s, d)
    gate_br, xb = norm_proj(x2d, g_norm, w, [(0, dr), (dr, dr)], [BF16, F32], epilogue=_rglru_epilogue)
    y = rglru_core(gate_br.reshape(b, s, dr), xb.reshape(b, s, dr), conv_w, conv_b, w_gates, b_gates, lru)
    return y.reshape(b * s, dr)


def ffn_layer(y2d, w_mix, h3d, g_norm, w_in_bf16, w_out_bf16, layer):
    b, s, d = h3d.shape
    return mix_out_ffn(y2d, w_mix.astype(BF16), h3d.reshape(b * s, d), g_norm, w_in_bf16, w_out_bf16,
                       layer).reshape(b, s, d)


def kernel(x, positions, norm_mix, norm_ffn, ffn_w_in, ffn_w_out, a_w_in, a_b_gates, a_g_out, a_w_out, b_w_in, b_g_q, b_g_k, b_w_out, c_w_in, c_g_out, c_w_out, d_w_in, d_conv_w, d_conv_b, d_w_gates, d_b_gates, d_lru, d_w_out):
    h = x
    depth = norm_mix.shape[0]
    ffn_wi, ffn_wo = ffn_w_in.astype(BF16), ffn_w_out.astype(BF16)
    for layer in range(depth):
        mixer, r = layer % 4, layer // 4
        if mixer == 0:
            y, w_mix = mixer_a(h, norm_mix[layer], a_w_in[r], a_b_gates[r], a_g_out[r]), a_w_out[r]
        elif mixer == 1:
            y, w_mix = mixer_b(h, positions, norm_mix[layer], b_w_in[r], b_g_q[r], b_g_k[r]), b_w_out[r]
        elif mixer == 2:
            y, w_mix = mixer_c(h, positions, norm_mix[layer], c_w_in[r], c_g_out[r]), c_w_out[r]
        else:
            y, w_mix = mixer_d(h, norm_mix[layer], d_w_in[r], d_conv_w[r], d_conv_b[r], d_w_gates[r], d_b_gates[r],
                               d_lru[r]), d_w_out[r]
        h = ffn_layer(y, w_mix, h, norm_ffn[layer], ffn_wi, ffn_wo, layer)
    return h
```
